```python
import math
import jax
import jax.numpy as jnp
from jax import lax
import numpy as np

D_MODEL = 4096
BATCH = 32
SEQ = 256
DEPTH = 4
DEC_BATCH = 2
DEC_SEQ = 4096
PAST_LEN = 512

GRID_W = 64
RMS_EPS = 1e-6
N_EVEN = (DEPTH + 1) // 2
N_ODD = DEPTH // 2
D_MIX = D_MODEL
HEAD_DIM = 128
D_A = D_MIX // 2
RG_BLOCK = 128
RG_HEADS = D_A // RG_BLOCK
RG_C = 8.0
RG_CONV = 4
H_B = (D_MIX // 2) // HEAD_DIM
HD_B = HEAD_DIM
D_B = H_B * HD_B
NA_ROWS = 8
NA_COLS = 16
H_C = (D_MIX // 2) // HEAD_DIM
HD_C = HEAD_DIM
KVH_C = H_C // 4
GQA_GROUP = H_C // KVH_C
D_C = H_C * HD_C
SWA_WINDOW = 128
SWA_BLOCK = 128
ROPE_BASE = 10000.0
D_SSD = D_MIX // 2
SSD_HEADDIM = 64
H_D = D_SSD // SSD_HEADDIM
SSD_GROUPS = 4
SSD_HPG = H_D // SSD_GROUPS
SSD_STATE = 128
SSD_CONV = 4
SSD_CHUNK = 128
D_XBC = D_SSD + 2 * SSD_GROUPS * SSD_STATE
D_FF = ((8 * D_MODEL // 3 + 255) // 256) * 256
FFN_CONV = 3
EVEN_IN = 2 * D_A + 3 * D_B
ODD_IN = D_C + 2 * KVH_C * HD_C + D_SSD + D_XBC + H_D
EVEN_SPLITS = (D_A, 2 * D_A, 2 * D_A + D_B, 2 * D_A + 2 * D_B)
ODD_SPLITS = (D_C, D_C + KVH_C * HD_C, D_C + 2 * KVH_C * HD_C,
              D_C + 2 * KVH_C * HD_C + D_SSD, D_C + 2 * KVH_C * HD_C + D_SSD + D_XBC)

kernel_name = 'hybrid_diffusion_trunk_step'


def rmsnorm(x, g):
    xf = x.astype(jnp.float32)
    y = xf * lax.rsqrt(jnp.mean(xf * xf, axis=-1, keepdims=True) + RMS_EPS)
    return (y * g.astype(jnp.float32)).astype(x.dtype)


def adaln(x, g, shift, scale):
    return rmsnorm(x, g) * (1 + scale) + shift


def modulation(cvec, w_mod, b_mod):
    m = (jax.nn.silu(cvec) @ w_mod + b_mod)[:, None, :]
    return jnp.split(m, 6, axis=-1)


def dwconv(x, w, b):
    k, ch = w.shape
    left = k // 2
    y = lax.conv_general_dilated(x, w[:, None, :].astype(x.dtype), window_strides=(1,),
                                 padding=[(left, k - 1 - left)],
                                 dimension_numbers=('NWC', 'WIO', 'NWC'),
                                 feature_group_count=ch)
    return y + b


def axial_rope(T, hd, dtype):
    t = jnp.arange(T)
    row = (t // GRID_W).astype(jnp.float32)
    col = (t % GRID_W).astype(jnp.float32)
    n = hd // 4
    inv = ROPE_BASE ** (-jnp.arange(n, dtype=jnp.float32) / n)
    ang = jnp.concatenate([row[:, None] * inv, col[:, None] * inv], axis=-1)
    return jnp.cos(ang).astype(dtype), jnp.sin(ang).astype(dtype)


def apply_rope(x, cos, sin):
    x1, x2 = jnp.split(x, 2, axis=-1)
    c = cos[None, :, None, :]
    s = sin[None, :, None, :]
    return jnp.concatenate([x1 * c - x2 * s, x1 * s + x2 * c], axis=-1)


def _affine_combine(left, right):
    a_l, b_l = left
    a_r, b_r = right
    return a_l * a_r, a_r * b_l + b_r


def scan_from(a, b, h0):
    a_cum, b_cum = lax.associative_scan(_affine_combine, (a, b), axis=1)
    return a_cum * h0[:, None] + b_cum


def rglru_coeffs(xc, w_a, b_a, w_i, b_i, lam):
    bn, L, _ = xc.shape
    xb = xc.reshape(bn, L, RG_HEADS, RG_BLOCK)
    r = jax.nn.sigmoid(jnp.einsum('blhi,hij->blhj', xb, w_a.astype(jnp.float32)).reshape(bn, L, D_A) + b_a)
    i = jax.nn.sigmoid(jnp.einsum('blhi,hij->blhj', xb, w_i.astype(jnp.float32)).reshape(bn, L, D_A) + b_i)
    log_a = -RG_C * r * jax.nn.softplus(-lam.astype(jnp.float32))
    return jnp.exp(log_a), jnp.sqrt(-jnp.expm1(2.0 * log_a)) * (i * xc)


def rglru_mixer(ga, xa, h0, conv_w, conv_b, w_a, b_a, w_i, b_i, lam):
    xc = dwconv(xa, conv_w, conv_b).astype(jnp.float32)
    a_f, u_f = rglru_coeffs(xc, w_a[0], b_a[0], w_i[0], b_i[0], lam[0])
    a_b, u_b = rglru_coeffs(xc, w_a[1], b_a[1], w_i[1], b_i[1], lam[1])
    h0 = h0.astype(jnp.float32)
    h_f = scan_from(a_f, u_f, h0[:, 0])
    h_b = scan_from(a_b[:, ::-1], u_b[:, ::-1], h0[:, 1])
    y = (h_f + h_b[:, ::-1]).astype(xa.dtype) * jax.nn.gelu(ga)
    return y, h_f, h_b


def context_attention(q, k, v):
    s = jnp.einsum('blhd,bhmd->bhlm', q, k).astype(jnp.float32)
    p = jax.nn.softmax(s, axis=-1).astype(v.dtype)
    return jnp.einsum('bhlm,bhmd->blhd', p, v)


def neighbourhood_attention(q, k, v, kc, vc, rel_bias):
    bn, T = q.shape[:2]
    rows = T // GRID_W
    kr = min(NA_ROWS, rows)
    qg = q.reshape(bn, rows, GRID_W, H_B, HD_B)
    kg = k.reshape(bn, rows, GRID_W, H_B, HD_B)
    vg = v.reshape(bn, rows, GRID_W, H_B, HD_B)
    cols = jnp.arange(GRID_W)
    col_idx = jnp.clip(cols - NA_COLS // 2, 0, GRID_W - NA_COLS)[:, None] + jnp.arange(NA_COLS)[None, :]
    bias_cols = rel_bias[:, :, col_idx - cols[:, None] + NA_COLS - 1]
    row_start = jnp.clip(jnp.arange(rows) - kr // 2, 0, rows - kr)

    def one_row(r):
        rs = row_start[r]
        kw = lax.dynamic_slice_in_dim(kg, rs, kr, axis=1)[:, :, col_idx]
        vw = lax.dynamic_slice_in_dim(vg, rs, kr, axis=1)[:, :, col_idx]
        qr = lax.dynamic_index_in_dim(qg, r, axis=1, keepdims=False)
        bias = bias_cols[:, rs + jnp.arange(kr) - r + NA_ROWS - 1].transpose(0, 2, 1, 3)
        s_loc = jnp.einsum('bchd,bkcjhd->bhckj', qr, kw).astype(jnp.float32) + bias.astype(jnp.float32)
        s_ctx = jnp.einsum('bchd,bhld->bhcl', qr, kc).astype(jnp.float32)
        s = jnp.concatenate([s_loc.reshape(bn, H_B, GRID_W, kr * NA_COLS), s_ctx], axis=-1)
        p = jax.nn.softmax(s, axis=-1).astype(v.dtype)
        p_loc = p[..., :kr * NA_COLS].reshape(bn, H_B, GRID_W, kr, NA_COLS)
        p_ctx = p[..., kr * NA_COLS:]
        return (jnp.einsum('bhckj,bkcjhd->bchd', p_loc, vw)
                + jnp.einsum('bhcl,bhld->bchd', p_ctx, vc))

    out = lax.map(one_row, jnp.arange(rows))
    return jnp.moveaxis(out, 0, 1).reshape(bn, T, D_B)


def swa_context(q, kc, vc, sink):
    bn, L = q.shape[:2]
    qg = q.reshape(bn, L, KVH_C, GQA_GROUP, HD_C)
    s = jnp.einsum('blkgd,bkmd->bkglm', qg, kc).astype(jnp.float32)
    s_sink = jnp.broadcast_to(sink.astype(jnp.float32).reshape(KVH_C, GQA_GROUP)[None, :, :, None, None],
                              s.shape[:-1] + (1,))
    p = jax.nn.softmax(jnp.concatenate([s, s_sink], axis=-1), axis=-1)[..., :L].astype(vc.dtype)
    return jnp.einsum('bkglm,bkmd->blkgd', p, vc).reshape(bn, L, D_C)


def swa_latent(q, k, v, kc, vc, sink):
    bn, T = q.shape[:2]
    nb = T // SWA_BLOCK
    lc = kc.shape[2]
    qb = q.reshape(bn, nb, SWA_BLOCK, KVH_C, GQA_GROUP, HD_C)

    def band(t):
        tp = jnp.pad(t, ((0, 0), (SWA_BLOCK, SWA_BLOCK), (0, 0), (0, 0)))
        tp = tp.reshape(bn, nb + 2, SWA_BLOCK, KVH_C, HD_C)
        return jnp.concatenate([tp[:, :-2], tp[:, 1:-1], tp[:, 2:]], axis=2)

    kb, vb = band(k), band(v)
    i = jnp.arange(SWA_BLOCK)[:, None]
    j = jnp.arange(3 * SWA_BLOCK)[None, :]
    kpos = jnp.arange(nb)[:, None, None] * SWA_BLOCK - SWA_BLOCK + j[None]
    mask = (jnp.abs(j - SWA_BLOCK - i) <= SWA_WINDOW)[None] & (kpos >= 0) & (kpos < T)
    s_loc = jnp.einsum('bnqkgd,bnjkd->bkgnqj', qb, kb).astype(jnp.float32)
    s_loc = jnp.where(mask, s_loc, -jnp.inf)
    s_ctx = jnp.einsum('bnqkgd,bkld->bkgnql', qb, kc).astype(jnp.float32)
    s_sink = jnp.broadcast_to(sink.astype(jnp.float32).reshape(KVH_C, GQA_GROUP)[None, :, :, None, None, None],
                              s_loc.shape[:-1] + (1,))
    p = jax.nn.softmax(jnp.concatenate([s_loc, s_ctx, s_sink], axis=-1), axis=-1).astype(v.dtype)
    p_loc = p[..., :3 * SWA_BLOCK]
    p_ctx = p[..., 3 * SWA_BLOCK:3 * SWA_BLOCK + lc]
    out = (jnp.einsum('bkgnqj,bnjkd->bnqkgd', p_loc, vb)
           + jnp.einsum('bkgnql,bkld->bnqkgd', p_ctx, vc))
    return out.reshape(bn, T, D_C)


def ssd_scan(x, dt, a, bm, cm, h0):
    bn, L = x.shape[:2]
    nc = L // SSD_CHUNK
    Q = SSD_CHUNK
    xc = x.reshape(bn, nc, Q, SSD_GROUPS, SSD_HPG, SSD_HEADDIM)
    dtc = dt.reshape(bn, nc, Q, SSD_GROUPS, SSD_HPG)
    bc = bm.reshape(bn, nc, Q, SSD_GROUPS, SSD_STATE)
    cc = cm.reshape(bn, nc, Q, SSD_GROUPS, SSD_STATE)
    acs = jnp.cumsum(dtc * a, axis=2)
    tri = jnp.tril(jnp.ones((Q, Q), dtype=bool))
    seg = acs[:, :, :, None] - acs[:, :, None, :]
    decay = jnp.exp(jnp.where(tri[:, :, None, None], seg, -jnp.inf))
    cb = jnp.einsum('bcign,bcjgn->bcijg', cc, bc)
    w = cb[..., None] * decay * dtc[:, :, None]
    y = jnp.einsum('bcijgh,bcjghp->bcighp', w, xc)
    to_end = jnp.exp(acs[:, :, -1:] - acs) * dtc
    states = jnp.einsum('bcjgn,bcjgh,bcjghp->bcghpn', bc, to_end, xc)
    chunk_decay = jnp.exp(acs[:, :, -1])

    def step(h, inp):
        dec, st = inp
        return dec[..., None, None] * h + st, h

    h_last, h_prev = lax.scan(step, h0, (jnp.moveaxis(chunk_decay, 1, 0), jnp.moveaxis(states, 1, 0)))
    h_prev = jnp.moveaxis(h_prev, 0, 1)
    y = y + jnp.einsum('bcign,bcghpn->bcighp', cc, h_prev) * jnp.exp(acs)[..., None]
    return y.reshape(bn, L, SSD_GROUPS, SSD_HPG, SSD_HEADDIM), h_last


def ssd_mixer(z, xbc, dt_raw, h0, conv_w, conv_b, a_log, dt_bias, d_skip, norm_g):
    bn, L, _ = z.shape
    xbc = jax.nn.silu(dwconv(xbc, conv_w, conv_b)).astype(jnp.float32)
    xs, bm, cm = jnp.split(xbc, (D_SSD, D_SSD + SSD_GROUPS * SSD_STATE), axis=-1)
    x = xs.reshape(bn, L, SSD_GROUPS, SSD_HPG, SSD_HEADDIM)
    bm = bm.reshape(bn, L, SSD_GROUPS, SSD_STATE)
    cm = cm.reshape(bn, L, SSD_GROUPS, SSD_STATE)
    h0 = h0.astype(jnp.float32).reshape(bn, 2, SSD_GROUPS, SSD_HPG, SSD_HEADDIM, SSD_STATE)
    dt_raw = dt_raw.astype(jnp.float32)

    def direction(d, xd, bd, cd, dtd):
        dt = jax.nn.softplus(dtd + dt_bias[d].astype(jnp.float32)).reshape(bn, L, SSD_GROUPS, SSD_HPG)
        a = -jnp.exp(a_log[d].astype(jnp.float32)).reshape(SSD_GROUPS, SSD_HPG)
        return ssd_scan(xd, dt, a, bd, cd, h0[:, d])

    y_f, h_f = direction(0, x, bm, cm, dt_raw)
    y_b, h_b = direction(1, x[:, ::-1], bm[:, ::-1], cm[:, ::-1], dt_raw[:, ::-1])
    y = y_f + y_b[:, ::-1] + d_skip.astype(jnp.float32).reshape(SSD_GROUPS, SSD_HPG)[:, :, None] * x
    y = y.reshape(bn, L, D_SSD).astype(z.dtype)
    y = rmsnorm(y * jax.nn.silu(z), norm_g)
    shape = (bn, H_D, SSD_HEADDIM, SSD_STATE)
    return y, h_f.reshape(shape), h_b.reshape(shape)


def even_mixer_context(h, w_in, rg):
    bn, L, _ = h.shape
    ga, xa, q, k, v = jnp.split(h @ w_in, EVEN_SPLITS, axis=-1)
    ya, h_f, h_b = rglru_mixer(ga, xa, jnp.zeros((bn, 2, D_A), jnp.float32), *rg)
    q = q.reshape(bn, L, H_B, HD_B) * HD_B ** -0.5
    k = k.reshape(bn, L, H_B, HD_B).transpose(0, 2, 1, 3)
    v = v.reshape(bn, L, H_B, HD_B).transpose(0, 2, 1, 3)
    yb = context_attention(q, k, v).reshape(bn, L, D_B)
    state = jnp.stack([h_f[:, -1], h_b[:, -1]], axis=1).astype(h.dtype)
    return jnp.concatenate([ya, yb], axis=-1), k, v, state


def even_mixer_latent(h, w_in, rg, rel_bias, kc, vc, h0):
    bn, T, _ = h.shape
    ga, xa, q, k, v = jnp.split(h @ w_in, EVEN_SPLITS, axis=-1)
    ya, _, _ = rglru_mixer(ga, xa, h0, *rg)
    q = q.reshape(bn, T, H_B, HD_B) * HD_B ** -0.5
    k = k.reshape(bn, T, H_B, HD_B)
    v = v.reshape(bn, T, H_B, HD_B)
    yb = neighbourhood_attention(q, k, v, kc, vc, rel_bias)
    return jnp.concatenate([ya, yb], axis=-1)


def odd_mixer_context(h, w_in, ssd, sink):
    bn, L, _ = h.shape
    q, k, v, z, xbc, dt = jnp.split(h @ w_in, ODD_SPLITS, axis=-1)
    q = q.reshape(bn, L, H_C, HD_C) * HD_C ** -0.5
    k = k.reshape(bn, L, KVH_C, HD_C).transpose(0, 2, 1, 3)
    v = v.reshape(bn, L, KVH_C, HD_C).transpose(0, 2, 1, 3)
    yc = swa_context(q, k, v, sink)
    h0 = jnp.zeros((bn, 2, H_D, SSD_HEADDIM, SSD_STATE), jnp.float32)
    yd, h_f, h_b = ssd_mixer(z, xbc, dt, h0, *ssd)
    state = jnp.stack([h_f, h_b], axis=1).astype(h.dtype)
    return jnp.concatenate([yc, yd], axis=-1), k, v, state


def odd_mixer_latent(h, w_in, ssd, sink, kc, vc, h0, cos, sin):
    bn, T, _ = h.shape
    q, k, v, z, xbc, dt = jnp.split(h @ w_in, ODD_SPLITS, axis=-1)
    q = apply_rope(q.reshape(bn, T, H_C, HD_C), cos, sin) * HD_C ** -0.5
    k = apply_rope(k.reshape(bn, T, KVH_C, HD_C), cos, sin)
    v = v.reshape(bn, T, KVH_C, HD_C)
    yc = swa_latent(q, k, v, kc, vc, sink)
    yd, _, _ = ssd_mixer(z, xbc, dt, h0, *ssd)
    return jnp.concatenate([yc, yd], axis=-1)


def conv_ffn(h, w_up, conv_w, conv_b, w_down):
    u = dwconv(h @ w_up, conv_w, conv_b)
    g, v = jnp.split(u, 2, axis=-1)
    return (jax.nn.silu(g) * v) @ w_down


def setup_inputs(seed: int = 0) -> dict:
    key = jax.random.key(seed)
    ks = iter(jax.random.split(key, 48))
    f32 = jnp.float32

    def nrm(shape, s):
        return jax.random.normal(next(ks), shape, f32) * s

    def gain(shape):
        return 1.0 + nrm(shape, 0.02)

    u = jax.random.uniform(next(ks), (N_EVEN, 2, D_A), f32, 0.9, 0.999)
    a_base = u ** (1.0 / RG_C)
    rg_lambda = jnp.log(a_base) - jnp.log1p(-a_base)
    ssd_a_log = jnp.log(jax.random.uniform(next(ks), (N_ODD, 2, H_D), f32, 1.0, 16.0))
    dt0 = jnp.exp(jax.random.uniform(next(ks), (N_ODD, 2, H_D), f32, math.log(1e-3), math.log(1e-1)))
    ssd_dt_bias = dt0 + jnp.log(-jnp.expm1(-dt0))
    return {
        'x_prompt': nrm((BATCH, SEQ, D_MODEL), 1.0),
        'x_sample': nrm((DEC_BATCH, DEC_SEQ, D_MODEL), 1.0),
        'cache_na_k': nrm((DEC_BATCH, N_EVEN, H_B, PAST_LEN, HD_B), 1.0),
        'cache_na_v': nrm((DEC_BATCH, N_EVEN, H_B, PAST_LEN, HD_B), 1.0),
        'state_rglru': nrm((DEC_BATCH, N_EVEN, 2, D_A), 0.5),
        'cache_swa_k': nrm((DEC_BATCH, N_ODD, KVH_C, PAST_LEN, HD_C), 1.0),
        'cache_swa_v': nrm((DEC_BATCH, N_ODD, KVH_C, PAST_LEN, HD_C), 1.0),
        'state_ssd': nrm((DEC_BATCH, N_ODD, 2, H_D, SSD_HEADDIM, SSD_STATE), 0.05),
        'c': nrm((DEC_BATCH, D_MODEL), 1.0),
        'c_ctx': nrm((D_MODEL,), 1.0),
        'w_mod': nrm((DEPTH, D_MODEL, 6 * D_MODEL), 0.5 * D_MODEL ** -0.5),
        'b_mod': nrm((DEPTH, 6 * D_MODEL), 0.02),
        'norm_mix_pre': gain((DEPTH, D_MODEL)),
        'norm_mix_post': gain((DEPTH, D_MODEL)),
        'norm_ffn_pre': gain((DEPTH, D_MODEL)),
        'norm_ffn_post': gain((DEPTH, D_MODEL)),
        'w_in_even': nrm((N_EVEN, D_MODEL, EVEN_IN), D_MODEL ** -0.5),
        'rg_conv_w': nrm((N_EVEN, RG_CONV, D_A), RG_CONV ** -0.5),
        'rg_conv_b': nrm((N_EVEN, D_A), 0.02),
        'rg_w_a': nrm((N_EVEN, 2, RG_HEADS, RG_BLOCK, RG_BLOCK), RG_BLOCK ** -0.5),
        'rg_b_a': nrm((N_EVEN, 2, D_A), 0.02),
        'rg_w_i': nrm((N_EVEN, 2, RG_HEADS, RG_BLOCK, RG_BLOCK), RG_BLOCK ** -0.5),
        'rg_b_i': nrm((N_EVEN, 2, D_A), 0.02),
        'rg_lambda': rg_lambda,
        'na_rel_bias': nrm((N_EVEN, H_B, 2 * NA_ROWS - 1, 2 * NA_COLS - 1), 0.02),
        'w_in_odd': nrm((N_ODD, D_MODEL, ODD_IN), D_MODEL ** -0.5),
        'swa_sink': nrm((N_ODD, H_C), 0.5),
        'ssd_conv_w': nrm((N_ODD, SSD_CONV, D_XBC), SSD_CONV ** -0.5),
        'ssd_conv_b': nrm((N_ODD, D_XBC), 0.02),
        'ssd_a_log': ssd_a_log,
        'ssd_dt_bias': ssd_dt_bias,
        'ssd_d': gain((N_ODD, H_D)),
        'ssd_norm': gain((N_ODD, D_SSD)),
        'w_out': nrm((DEPTH, D_MIX, D_MODEL), D_MIX ** -0.5),
        'ffn_w_up': nrm((DEPTH, D_MODEL, 2 * D_FF), D_MODEL ** -0.5),
        'ffn_conv_w': nrm((DEPTH, FFN_CONV, 2 * D_FF), FFN_CONV ** -0.5),
        'ffn_conv_b': nrm((DEPTH, 2 * D_FF), 0.02),
        'ffn_w_down': nrm((DEPTH, D_FF, D_MODEL), D_FF ** -0.5),
    }


def reference(x_prompt, x_sample, cache_na_k, cache_na_v, state_rglru, cache_swa_k, cache_swa_v, state_ssd,
              c, c_ctx, w_mod, b_mod, norm_mix_pre, norm_mix_post, norm_ffn_pre, norm_ffn_post,
              w_in_even, rg_conv_w, rg_conv_b, rg_w_a, rg_b_a, rg_w_i, rg_b_i, rg_lambda, na_rel_bias,
              w_in_odd, swa_sink, ssd_conv_w, ssd_conv_b, ssd_a_log, ssd_dt_bias, ssd_d, ssd_norm,
              w_out, ffn_w_up, ffn_conv_w, ffn_conv_b, ffn_w_down):
    xc = x_prompt
    xl = x_sample
    T = xl.shape[1]
    cos, sin = axial_rope(T, HD_C, xl.dtype)
    new_na_k, new_na_v, new_rglru = [], [], []
    new_swa_k, new_swa_v, new_ssd = [], [], []
    for layer in range(DEPTH):
        mc = modulation(c_ctx[None], w_mod[layer], b_mod[layer])
        ml = modulation(c, w_mod[layer], b_mod[layer])
        hc = adaln(xc, norm_mix_pre[layer], mc[0], mc[1])
        hl = adaln(xl, norm_mix_pre[layer], ml[0], ml[1])
        if layer % 2 == 0:
            e = layer // 2
            rg = (rg_conv_w[e], rg_conv_b[e], rg_w_a[e], rg_b_a[e], rg_w_i[e], rg_b_i[e], rg_lambda[e])
            oc, k_ctx, v_ctx, st = even_mixer_context(hc, w_in_even[e], rg)
            ol = even_mixer_latent(hl, w_in_even[e], rg, na_rel_bias[e],
                                   cache_na_k[:, e], cache_na_v[:, e], state_rglru[:, e])
            new_na_k.append(k_ctx)
            new_na_v.append(v_ctx)
            new_rglru.append(st)
        else:
            o = layer // 2
            ssd = (ssd_conv_w[o], ssd_conv_b[o], ssd_a_log[o], ssd_dt_bias[o], ssd_d[o], ssd_norm[o])
            oc, k_ctx, v_ctx, st = odd_mixer_context(hc, w_in_odd[o], ssd, swa_sink[o])
            ol = odd_mixer_latent(hl, w_in_odd[o], ssd, swa_sink[o],
                                  cache_swa_k[:, o], cache_swa_v[:, o], state_ssd[:, o], cos, sin)
            new_swa_k.append(k_ctx)
            new_swa_v.append(v_ctx)
            new_ssd.append(st)
        xc = xc + mc[2] * rmsnorm(oc @ w_out[layer], norm_mix_post[layer])
        xl = xl + ml[2] * rmsnorm(ol @ w_out[layer], norm_mix_post[layer])
        hc = adaln(xc, norm_ffn_pre[layer], mc[3], mc[4])
        hl = adaln(xl, norm_ffn_pre[layer], ml[3], ml[4])
        fc = conv_ffn(hc, ffn_w_up[layer], ffn_conv_w[layer], ffn_conv_b[layer], ffn_w_down[layer])
        fl = conv_ffn(hl, ffn_w_up[layer], ffn_conv_w[layer], ffn_conv_b[layer], ffn_w_down[layer])
        xc = xc + mc[5] * rmsnorm(fc, norm_ffn_post[layer])
        xl = xl + ml[5] * rmsnorm(fl, norm_ffn_post[layer])
    return (xc, xl, jnp.stack(new_na_k, axis=1), jnp.stack(new_na_v, axis=1), jnp.stack(new_rglru, axis=1),
            jnp.stack(new_swa_k, axis=1), jnp.stack(new_swa_v, axis=1), jnp.stack(new_ssd, axis=1))
```

```python
import functools
import math

import jax
import jax.numpy as jnp
from jax import lax
from jax.experimental import pallas as pl
from jax.experimental.pallas import tpu as pltpu

F32 = jnp.float32
BF16 = jnp.bfloat16

RMS_EPS = 1e-6
HEAD_DIM = 128
GRID_W = 64
NA_ROWS = 8
NA_COLS = 16
NA_QROWS = 4
SWA_BLOCK = 128
SWA_WINDOW = 128
ROPE_BASE = 10000.0
RG_C = 8.0
SSD_CHUNK = 128
SSD_HEADDIM = 64
SSD_GROUP_HEADS = 8
MASKED = -1e30
MOD_ROWS = 8
V7X_VMEM_BYTES = 64 * 1024 * 1024
VMEM_LIMIT = V7X_VMEM_BYTES - 8 * 1024 * 1024
ROW_TILE = 1024
COL_TILE = 1024
EW_ROWS = 256
NT_DIMS = (((1,), (1,)), ((), ()))
TN_DIMS = (((0,), (0,)), ((), ()))


def _params(*sem):
    return pltpu.CompilerParams(dimension_semantics=sem, vmem_limit_bytes=VMEM_LIMIT)


def _silu(x):
    return x / (1.0 + jnp.exp(-x))


def _sigmoid(x):
    return 1.0 / (1.0 + jnp.exp(-x))


def _softplus(x):
    return jnp.maximum(x, 0.0) + jnp.log(1.0 + jnp.exp(-jnp.abs(x)))


def _gelu_tanh(x):
    return 0.5 * x * (1.0 + jnp.tanh(math.sqrt(2.0 / math.pi) * (x + 0.044715 * (x * x * x))))


def _rms(x, g):
    ms = jnp.mean(x * x, axis=-1, keepdims=True)
    return x * lax.rsqrt(ms + RMS_EPS) * g


def _dot(a, b):
    return jnp.dot(a, b, preferred_element_type=F32)


def _dot_nt(a, b):
    return lax.dot_general(a, b, NT_DIMS, preferred_element_type=F32)


def _mod_kernel(c_ref, w_ref, b_ref, o_ref):
    s = _silu(c_ref[...])
    o_ref[...] = _dot(s.astype(BF16), w_ref[...].astype(BF16)) + b_ref[...]


def modulation_all(c8, w_mod, b_mod):
    depth, d, n6 = w_mod.shape
    tn = 512
    per = d // tn
    return pl.pallas_call(
        _mod_kernel,
        grid=(depth, n6 // tn),
        in_specs=[
            pl.BlockSpec((MOD_ROWS, d), lambda l, j: (0, 0)),
            pl.BlockSpec((None, d, tn), lambda l, j: (l, 0, j)),
            pl.BlockSpec((None, 1, tn), lambda l, j: (l, 0, j)),
        ],
        out_specs=pl.BlockSpec((None, None, MOD_ROWS, tn), lambda l, j: (l, j // per, 0, j % per)),
        out_shape=jax.ShapeDtypeStruct((depth, 6, MOD_ROWS, d), F32),
        compiler_params=_params("parallel", "parallel"),
        name="modulation",
    )(c8, w_mod, b_mod.reshape(depth, 1, n6))


def _mod_row(i, tm, n_ctx, lat_len):
    row0 = i * tm
    return jnp.where(row0 < n_ctx, 0, 1 + (row0 - n_ctx) // lat_len)


def _adaln_kernel(x_ref, mod_ref, g_ref, h_ref, *, tm, n_ctx, lat_len, shift_i, scale_i):
    r = _mod_row(pl.program_id(0), tm, n_ctx, lat_len)
    shift = mod_ref[shift_i, pl.ds(r, 1), :]
    scale = mod_ref[scale_i, pl.ds(r, 1), :]
    h_ref[...] = (_rms(x_ref[...], g_ref[...]) * (1.0 + scale) + shift).astype(h_ref.dtype)


def adaln(x, mod, g, *, n_ctx, lat_len, shift_i, scale_i):
    m, d = x.shape
    tm = EW_ROWS
    kern = functools.partial(_adaln_kernel, tm=tm, n_ctx=n_ctx, lat_len=lat_len,
                             shift_i=shift_i, scale_i=scale_i)
    return pl.pallas_call(
        kern,
        grid=(m // tm,),
        in_specs=[
            pl.BlockSpec((tm, d), lambda i: (i, 0)),
            pl.BlockSpec((6, MOD_ROWS, d), lambda i: (0, 0, 0)),
            pl.BlockSpec((1, d), lambda i: (0, 0)),
        ],
        out_specs=pl.BlockSpec((tm, d), lambda i: (i, 0)),
        out_shape=jax.ShapeDtypeStruct((m, d), BF16),
        compiler_params=_params("parallel"),
        name="adaln",
    )(x, mod, g.reshape(1, d))


def _resid_kernel(x_ref, y_ref, mod_ref, gpost_ref, gpre_ref, modn_ref, xo_ref, h_ref, *,
                  tm, n_ctx, lat_len, gate_i, shift_i, scale_i):
    r = _mod_row(pl.program_id(0), tm, n_ctx, lat_len)
    gate = mod_ref[gate_i, pl.ds(r, 1), :]
    x = x_ref[...] + gate * _rms(y_ref[...], gpost_ref[...])
    xo_ref[...] = x
    shift = modn_ref[shift_i, pl.ds(r, 1), :]
    scale = modn_ref[scale_i, pl.ds(r, 1), :]
    h_ref[...] = (_rms(x, gpre_ref[...]) * (1.0 + scale) + shift).astype(h_ref.dtype)


def resid_adaln(x, y, mod, gpost, gpre, modn, *, n_ctx, lat_len, gate_i, shift_i, scale_i):
    m, d = x.shape
    tm = EW_ROWS
    kern = functools.partial(_resid_kernel, tm=tm, n_ctx=n_ctx, lat_len=lat_len,
                             gate_i=gate_i, shift_i=shift_i, scale_i=scale_i)
    row = pl.BlockSpec((tm, d), lambda i: (i, 0))
    vec = pl.BlockSpec((1, d), lambda i: (0, 0))
    modspec = pl.BlockSpec((6, MOD_ROWS, d), lambda i: (0, 0, 0))
    return pl.pallas_call(
        kern,
        grid=(m // tm,),
        in_specs=[row, row, modspec, vec, vec, modspec],
        out_specs=[row, row],
        out_shape=[jax.ShapeDtypeStruct((m, d), F32), jax.ShapeDtypeStruct((m, d), BF16)],
        compiler_params=_params("parallel"),
        name="resid_adaln",
    )(x, y, mod, gpost.reshape(1, d), gpre.reshape(1, d), modn)


def _mm_kernel(a_ref, w_ref, o_ref):
    o_ref[...] = _dot(a_ref[...], w_ref[...]).astype(o_ref.dtype)


def matmul(a, w, out_dtype=F32):
    m, k = a.shape
    n = w.shape[1]
    tm, tn = min(ROW_TILE, m), min(COL_TILE, n)
    return pl.pallas_call(
        _mm_kernel,
        grid=(m // tm, n // tn),
        in_specs=[pl.BlockSpec((tm, k), lambda i, j: (i, 0)),
                  pl.BlockSpec((k, tn), lambda i, j: (0, j))],
        out_specs=pl.BlockSpec((tm, tn), lambda i, j: (i, j)),
        out_shape=jax.ShapeDtypeStruct((m, n), out_dtype),
        compiler_params=_params("parallel", "parallel"),
        name="matmul",
    )(a, w)


def _mm_acc_kernel(a_ref, w_ref, o_ref):
    @pl.when(pl.program_id(2) == 0)
    def _():
        o_ref[...] = jnp.zeros_like(o_ref)

    o_ref[...] += _dot(a_ref[...], w_ref[...])


def matmul_ktiled(a, w, tk):
    m, k = a.shape
    n = w.shape[1]
    tm, tn = min(ROW_TILE, m), min(COL_TILE, n)
    return pl.pallas_call(
        _mm_acc_kernel,
        grid=(m // tm, n // tn, k // tk),
        in_specs=[pl.BlockSpec((tm, tk), lambda i, j, kk: (i, kk)),
                  pl.BlockSpec((tk, tn), lambda i, j, kk: (kk, j))],
        out_specs=pl.BlockSpec((tm, tn), lambda i, j, kk: (i, j)),
        out_shape=jax.ShapeDtypeStruct((m, n), F32),
        compiler_params=_params("parallel", "parallel", "arbitrary"),
        name="matmul_ktiled",
    )(a, w)


def _dt_kernel(h_ref, w_ref, wt_ref, oc_ref, or_ref):
    h = h_ref[...]
    oc_ref[...] = _dot(h, w_ref[...])
    or_ref[...] = _dot_nt(wt_ref[...], h)


def dt_proj(h, w_pad, w_t):
    m, k = h.shape
    nh = w_t.shape[0]
    tm = min(ROW_TILE, m)
    return pl.pallas_call(
        _dt_kernel,
        grid=(m // tm,),
        in_specs=[pl.BlockSpec((tm, k), lambda i: (i, 0)),
                  pl.BlockSpec((k, 128), lambda i: (0, 0)),
                  pl.BlockSpec((nh, k), lambda i: (0, 0))],
        out_specs=[pl.BlockSpec((tm, 128), lambda i: (i, 0)),
                   pl.BlockSpec((nh, tm), lambda i: (0, i))],
        out_shape=[jax.ShapeDtypeStruct((m, 128), F32), jax.ShapeDtypeStruct((nh, m), F32)],
        compiler_params=_params("parallel"),
        name="dt_proj",
    )(h, w_pad, w_t)


FFN_HALO = 16


def _ffn_up_kernel(a_ref, ap_ref, an_ref, wg_ref, wv_ref, cg_ref, cv_ref, o_ref, aext, *,
                   tm, n_ctx, ctx_len, lat_len):
    i = pl.program_id(0)

    @pl.when(pl.program_id(1) == 0)
    def _():
        aext[0:FFN_HALO, :] = ap_ref[...]
        aext[FFN_HALO:FFN_HALO + tm, :] = a_ref[...]
        aext[FFN_HALO + tm:, :] = an_ref[...]

    a = aext[...]
    g = _dot(a, wg_ref[...])
    v = _dot(a, wv_ref[...])
    row = i * tm + lax.broadcasted_iota(jnp.int32, (tm, 1), 0)
    is_ctx = i * tm < n_ctx
    seq_len = jnp.where(is_ctx, ctx_len, lat_len)
    pos = lax.rem(row - jnp.where(is_ctx, 0, n_ctx), seq_len)
    has_prev = pos != 0
    has_next = pos != seq_len - 1
    rows = tm + 2 * FFN_HALO

    def conv(u, c_ref):
        up = pltpu.roll(u, 1, 0)[FFN_HALO:FFN_HALO + tm]
        un = pltpu.roll(u, rows - 1, 0)[FFN_HALO:FFN_HALO + tm]
        uc = u[FFN_HALO:FFN_HALO + tm]
        return (c_ref[0:1, :] * jnp.where(has_prev, up, 0.0) + c_ref[1:2, :] * uc
                + c_ref[2:3, :] * jnp.where(has_next, un, 0.0) + c_ref[3:4, :])

    o_ref[...] = (_silu(conv(g, cg_ref)) * conv(v, cv_ref)).astype(o_ref.dtype)


def ffn_up(h, wg, wv, cg, cv, *, n_ctx, ctx_len, lat_len):
    m, k = h.shape
    n = wg.shape[1]
    tm = min(ROW_TILE, m)
    tn = 512
    nhalo = m // FFN_HALO
    per = tm // FFN_HALO
    kern = functools.partial(_ffn_up_kernel, tm=tm, n_ctx=n_ctx, ctx_len=ctx_len, lat_len=lat_len)
    return pl.pallas_call(
        kern,
        grid=(m // tm, n // tn),
        in_specs=[
            pl.BlockSpec((tm, k), lambda i, j: (i, 0)),
            pl.BlockSpec((FFN_HALO, k), lambda i, j: (jnp.maximum(i * per - 1, 0), 0)),
            pl.BlockSpec((FFN_HALO, k), lambda i, j: (jnp.minimum((i + 1) * per, nhalo - 1), 0)),
            pl.BlockSpec((k, tn), lambda i, j: (0, j)),
            pl.BlockSpec((k, tn), lambda i, j: (0, j)),
            pl.BlockSpec((8, tn), lambda i, j: (0, j)),
            pl.BlockSpec((8, tn), lambda i, j: (0, j)),
        ],
        out_specs=pl.BlockSpec((tm, tn), lambda i, j: (i, j)),
        out_shape=jax.ShapeDtypeStruct((m, n), BF16),
        scratch_shapes=[pltpu.VMEM((tm + 2 * FFN_HALO, k), BF16)],
        compiler_params=_params("parallel", "arbitrary"),
        name="ffn_up",
    )(h, h, h, wg, wv, cg, cv)


RG_CHUNK = 256
RG_PAD = 8


def _scan_chunk(a, u, h, reverse):
    nblk = a.shape[0] // 8
    row8 = lax.broadcasted_iota(jnp.int32, (8, a.shape[1]), 0)
    outs = [None] * nblk
    for b in (range(nblk - 1, -1, -1) if reverse else range(nblk)):
        aa = a[8 * b:8 * b + 8]
        uu = u[8 * b:8 * b + 8]
        for s in (1, 2, 4):
            keep = (row8 < 8 - s) if reverse else (row8 >= s)
            shift = 8 - s if reverse else s
            a_sh = jnp.where(keep, pltpu.roll(aa, shift, 0), 1.0)
            u_sh = jnp.where(keep, pltpu.roll(uu, shift, 0), 0.0)
            uu = aa * u_sh + uu
            aa = aa * a_sh
        hb = aa * h + uu
        outs[b] = hb
        h = hb[0:1] if reverse else hb[7:8]
    return jnp.concatenate(outs, axis=0), h


def _rglru_kernel(ga_ref, xa_ref, h0_ref, cw_ref, cb_ref, wa_ref, ba_ref, wi_ref, bi_ref, lam_ref,
                  y_ref, st_ref, xpad, hf, *, seq, heads):
    chunk = min(RG_CHUNK, seq)
    nchunks = seq // chunk
    width = heads * HEAD_DIM
    xpad[0:RG_PAD, :] = jnp.zeros((RG_PAD, width), F32)
    xpad[RG_PAD + seq:, :] = jnp.zeros((RG_PAD, width), F32)
    xpad[RG_PAD:RG_PAD + seq, :] = xa_ref[...]
    win_rows = chunk + 2 * RG_PAD

    for hh in range(heads):
        lanes = slice(hh * HEAD_DIM, (hh + 1) * HEAD_DIM)

        def coeffs(t0, d, lanes=lanes, hh=hh):
            win = xpad[pl.ds(t0, win_rows), lanes]

            def at(off):
                return pltpu.roll(win, (win_rows - off) % win_rows, 0)[RG_PAD:RG_PAD + chunk]

            xc = (cw_ref[0:1, lanes] * at(-2) + cw_ref[1:2, lanes] * at(-1)
                  + cw_ref[2:3, lanes] * at(0) + cw_ref[3:4, lanes] * at(1) + cb_ref[0:1, lanes])
            xb = xc.astype(BF16)
            r = _sigmoid(_dot(xb, wa_ref[d, hh]) + ba_ref[d:d + 1, lanes])
            gate_i = _sigmoid(_dot(xb, wi_ref[d, hh]) + bi_ref[d:d + 1, lanes])
            log_a = (-RG_C) * r * _softplus(-lam_ref[d:d + 1, lanes])
            a = jnp.exp(log_a)
            u = jnp.sqrt(1.0 - jnp.exp(2.0 * log_a)) * (gate_i * xc)
            return a, u

        def fwd(c, h, lanes=lanes, coeffs=coeffs):
            t0 = pl.multiple_of(c * chunk, chunk)
            a, u = coeffs(t0, 0)
            hs, h = _scan_chunk(a, u, h, reverse=False)
            hf[pl.ds(t0, chunk), lanes] = hs
            return h

        h_f = lax.fori_loop(0, nchunks, fwd, h0_ref[0:1, lanes])

        def bwd(cc, h, lanes=lanes, coeffs=coeffs):
            t0 = pl.multiple_of((nchunks - 1 - cc) * chunk, chunk)
            a, u = coeffs(t0, 1)
            hs, h = _scan_chunk(a, u, h, reverse=True)
            y = (hf[pl.ds(t0, chunk), lanes] + hs) * _gelu_tanh(ga_ref[pl.ds(t0, chunk), lanes])
            y_ref[pl.ds(t0, chunk), lanes] = y.astype(y_ref.dtype)
            return h

        h_b = lax.fori_loop(0, nchunks, bwd, h0_ref[1:2, lanes])
        st_ref[0:1, lanes] = h_f
        st_ref[1:2, lanes] = h_b


def rglru_mixer(proj, h0, conv_w, conv_b, w_a, b_a, w_i, b_i, lam, *, row0, nseq, seq, d_a, heads):
    width = heads * HEAD_DIM
    ncol = d_a // width
    rb0 = row0 // seq
    kern = functools.partial(_rglru_kernel, seq=seq, heads=heads)
    vec2 = pl.BlockSpec((2, width), lambda b, j: (0, j))
    return pl.pallas_call(
        kern,
        grid=(nseq, ncol),
        in_specs=[
            pl.BlockSpec((seq, width), lambda b, j: (rb0 + b, j)),
            pl.BlockSpec((seq, width), lambda b, j: (rb0 + b, ncol + j)),
            pl.BlockSpec((None, 2, width), lambda b, j: (b, 0, j)),
            pl.BlockSpec((4, width), lambda b, j: (0, j)),
            pl.BlockSpec((1, width), lambda b, j: (0, j)),
            pl.BlockSpec((2, heads, HEAD_DIM, HEAD_DIM), lambda b, j: (0, j, 0, 0)),
            vec2,
            pl.BlockSpec((2, heads, HEAD_DIM, HEAD_DIM), lambda b, j: (0, j, 0, 0)),
            vec2,
            vec2,
        ],
        out_specs=[pl.BlockSpec((seq, width), lambda b, j: (b, j)),
                   pl.BlockSpec((None, 2, width), lambda b, j: (b, 0, j))],
        out_shape=[jax.ShapeDtypeStruct((nseq * seq, d_a), BF16),
                   jax.ShapeDtypeStruct((nseq, 2, d_a), F32)],
        scratch_shapes=[pltpu.VMEM((seq + 2 * RG_PAD, width), F32), pltpu.VMEM((seq, width), F32)],
        compiler_params=_params("parallel", "parallel"),
        name="rglru",
    )(proj, proj, h0, conv_w, conv_b.reshape(1, d_a), w_a, b_a, w_i, b_i, lam)


def _sink_column(sink_ref, first_head, grp, rows_per_head):
    rows = lax.broadcasted_iota(jnp.int32, (grp * rows_per_head, 1), 0)
    col = jnp.zeros((grp * rows_per_head, 1), F32)
    for g in range(grp):
        in_g = (rows >= g * rows_per_head) & (rows < (g + 1) * rows_per_head)
        col = jnp.where(in_g, sink_ref[first_head + g], col)
    return col


def _ctx_attn_kernel(sink_ref, q_ref, k_ref, v_ref, o_ref, ko_ref, vo_ref, *, nkv, grp, use_sink):
    seq = q_ref.shape[0]
    scale = HEAD_DIM ** -0.5
    for kk in range(nkv):
        k = k_ref[:, kk * HEAD_DIM:(kk + 1) * HEAD_DIM]
        v = v_ref[:, kk * HEAD_DIM:(kk + 1) * HEAD_DIM]
        ko_ref[kk] = k
        vo_ref[kk] = v
        qs = jnp.concatenate(
            [q_ref[:, (kk * grp + g) * HEAD_DIM:(kk * grp + g + 1) * HEAD_DIM] for g in range(grp)], axis=0)
        s = _dot_nt((qs * scale).astype(BF16), k.astype(BF16))
        m = jnp.max(s, axis=-1, keepdims=True)
        if use_sink:
            first = (pl.program_id(1) * nkv + kk) * grp
            sk = _sink_column(sink_ref, first, grp, seq)
            m = jnp.maximum(m, sk)
        p = jnp.exp(s - m)
        den = jnp.sum(p, axis=-1, keepdims=True)
        if use_sink:
            den = den + jnp.exp(sk - m)
        o = _dot(p.astype(BF16), v.astype(BF16)) / den
        for g in range(grp):
            hq = kk * grp + g
            o_ref[:, hq * HEAD_DIM:(hq + 1) * HEAD_DIM] = o[g * seq:(g + 1) * seq].astype(o_ref.dtype)


def ctx_attention(proj, sink, *, nseq, seq, q_col, k_col, v_col, n_kv, grp, use_sink):
    nkv = max(1, 4 // grp)
    qw = nkv * grp * HEAD_DIM
    kw = nkv * HEAD_DIM
    kern = functools.partial(_ctx_attn_kernel, nkv=nkv, grp=grp, use_sink=use_sink)
    return pl.pallas_call(
        kern,
        grid=(nseq, n_kv // nkv),
        in_specs=[
            pl.BlockSpec(memory_space=pltpu.SMEM),
            pl.BlockSpec((seq, qw), lambda b, j: (b, q_col // qw + j)),
            pl.BlockSpec((seq, kw), lambda b, j: (b, k_col // kw + j)),
            pl.BlockSpec((seq, kw), lambda b, j: (b, v_col // kw + j)),
        ],
        out_specs=[
            pl.BlockSpec((seq, qw), lambda b, j: (b, j)),
            pl.BlockSpec((None, nkv, seq, HEAD_DIM), lambda b, j: (b, j, 0, 0)),
            pl.BlockSpec((None, nkv, seq, HEAD_DIM), lambda b, j: (b, j, 0, 0)),
        ],
        out_shape=[
            jax.ShapeDtypeStruct((nseq * seq, n_kv * grp * HEAD_DIM), BF16),
            jax.ShapeDtypeStruct((nseq, n_kv, seq, HEAD_DIM), F32),
            jax.ShapeDtypeStruct((nseq, n_kv, seq, HEAD_DIM), F32),
        ],
        compiler_params=_params("parallel", "parallel"),
        name="ctx_attention",
    )(sink, proj, proj, proj)


def na_bias_table(rel_bias):
    qi = jnp.arange(NA_QROWS)[:, None, None, None]
    c = jnp.arange(GRID_W)[None, :, None, None]
    ki = jnp.arange(3 * NA_QROWS)[None, None, :, None]
    j = jnp.arange(GRID_W)[None, None, None, :]
    cs = jnp.clip(c - NA_COLS // 2, 0, GRID_W - NA_COLS)
    col_ok = (j >= cs) & (j < cs + NA_COLS)
    d = ki - qi + (NA_ROWS - 1 - NA_QROWS)
    dc = j - c + NA_COLS - 1
    row_ok = jnp.stack([
        (ki >= NA_QROWS) & (ki < NA_QROWS + NA_ROWS) & (qi >= 0),
        (ki >= qi) & (ki < qi + NA_ROWS),
        (ki >= 0) & (ki < NA_ROWS) & (qi >= 0),
    ])
    ok = row_ok & col_ok[None]
    vals = rel_bias[:, jnp.clip(d, 0, 2 * NA_ROWS - 2), jnp.clip(dc, 0, 2 * NA_COLS - 2)]
    tab = jnp.where(ok[:, None], vals[None].astype(F32), MASKED)
    nh = rel_bias.shape[0]
    return tab.reshape(3, nh, NA_QROWS * GRID_W, 3 * NA_QROWS * GRID_W)


def _na_kernel(q_ref, kp_ref, kc_ref, kn_ref, vp_ref, vc_ref, vn_ref, kx_ref, vx_ref, bias_ref, o_ref):
    q = (q_ref[...] * HEAD_DIM ** -0.5).astype(BF16)
    kl = jnp.concatenate([kp_ref[...], kc_ref[...], kn_ref[...]], axis=0).astype(BF16)
    vl = jnp.concatenate([vp_ref[...], vc_ref[...], vn_ref[...]], axis=0).astype(BF16)
    s_loc = _dot_nt(q, kl) + bias_ref[...]
    s_ctx = _dot_nt(q, kx_ref[...].astype(BF16))
    m = jnp.maximum(jnp.max(s_loc, axis=-1, keepdims=True), jnp.max(s_ctx, axis=-1, keepdims=True))
    p_loc = jnp.exp(s_loc - m)
    p_ctx = jnp.exp(s_ctx - m)
    den = jnp.sum(p_loc, axis=-1, keepdims=True) + jnp.sum(p_ctx, axis=-1, keepdims=True)
    o = _dot(p_loc.astype(BF16), vl) + _dot(p_ctx.astype(BF16), vx_ref[...].astype(BF16))
    o_ref[...] = (o / den).astype(o_ref.dtype)


def na_attention(proj, k_ctx, v_ctx, bias_tab, *, row0, nseq, seq, q_col, k_col, v_col, n_heads):
    blk = NA_QROWS * GRID_W
    nb = seq // blk
    rb0 = row0 // blk
    past = k_ctx.shape[2]
    qc, kc, vc = q_col // HEAD_DIM, k_col // HEAD_DIM, v_col // HEAD_DIM

    def rows(b, i):
        return rb0 + b * nb + i

    def spec(col, delta):
        return pl.BlockSpec((blk, HEAD_DIM),
                            lambda h, b, i: (rows(b, jnp.clip(i + delta, 0, nb - 1)), col + h))

    def block_type(i):
        return jnp.where(i == 0, 0, jnp.where(i == nb - 1, 2, 1))

    cache = pl.BlockSpec((None, None, past, HEAD_DIM), lambda h, b, i: (b, h, 0, 0))
    return pl.pallas_call(
        _na_kernel,
        grid=(n_heads, nseq, nb),
        in_specs=[spec(qc, 0), spec(kc, -1), spec(kc, 0), spec(kc, 1),
                  spec(vc, -1), spec(vc, 0), spec(vc, 1), cache, cache,
                  pl.BlockSpec((None, None, blk, 3 * blk), lambda h, b, i: (block_type(i), h, 0, 0))],
        out_specs=pl.BlockSpec((blk, HEAD_DIM), lambda h, b, i: (b * nb + i, h)),
        out_shape=jax.ShapeDtypeStruct((nseq * seq, n_heads * HEAD_DIM), BF16),
        compiler_params=_params("parallel", "parallel", "parallel"),
        name="na_attention",
    )(proj, proj, proj, proj, proj, proj, proj, k_ctx, v_ctx, bias_tab)


def rope_tables(seq):
    t = jnp.arange(seq)
    row = (t // GRID_W).astype(F32)
    col = (t % GRID_W).astype(F32)
    n = HEAD_DIM // 4
    inv = ROPE_BASE ** (-jnp.arange(n, dtype=F32) / n)
    ang = jnp.concatenate([row[:, None] * inv, col[:, None] * inv], axis=-1)
    cos, sin = jnp.cos(ang), jnp.sin(ang)
    return jnp.concatenate([cos, cos], axis=-1), jnp.concatenate([-sin, sin], axis=-1)


def _rope(x, cos, sin):
    return x * cos + pltpu.roll(x, HEAD_DIM // 2, 1) * sin


def _swa_kernel(sink_ref, q_ref, kp_ref, kc_ref, kn_ref, vp_ref, vc_ref, vn_ref, kx_ref, vx_ref,
                cos_ref, sin_ref, o_ref, *, grp, nb):
    kvh = pl.program_id(0)
    i = pl.program_id(2)
    blk = SWA_BLOCK

    def tables(ib):
        t0 = pl.multiple_of(ib * blk, blk)
        return cos_ref[pl.ds(t0, blk), :], sin_ref[pl.ds(t0, blk), :]

    cq, sq = tables(i)
    cp, sp = tables(jnp.maximum(i - 1, 0))
    cn, sn = tables(jnp.minimum(i + 1, nb - 1))
    scale = HEAD_DIM ** -0.5
    qs = jnp.concatenate(
        [_rope(q_ref[:, g * HEAD_DIM:(g + 1) * HEAD_DIM], cq, sq) * scale for g in range(grp)], axis=0).astype(BF16)
    kl = jnp.concatenate([_rope(kp_ref[...], cp, sp), _rope(kc_ref[...], cq, sq),
                          _rope(kn_ref[...], cn, sn)], axis=0).astype(BF16)
    vl = jnp.concatenate([vp_ref[...], vc_ref[...], vn_ref[...]], axis=0).astype(BF16)
    qi = lax.broadcasted_iota(jnp.int32, (blk, 3 * blk), 0)
    kj = lax.broadcasted_iota(jnp.int32, (blk, 3 * blk), 1)
    ok = (jnp.abs(kj - blk - qi) <= SWA_WINDOW)
    ok = ok & ((kj >= blk) | (i > 0)) & ((kj < 2 * blk) | (i < nb - 1))
    okg = jnp.concatenate([ok] * grp, axis=0)
    s_loc = jnp.where(okg, _dot_nt(qs, kl), MASKED)
    s_ctx = _dot_nt(qs, kx_ref[...].astype(BF16))
    sk = _sink_column(sink_ref, kvh * grp, grp, blk)
    m = jnp.maximum(jnp.maximum(jnp.max(s_loc, axis=-1, keepdims=True),
                                jnp.max(s_ctx, axis=-1, keepdims=True)), sk)
    p_loc = jnp.exp(s_loc - m)
    p_ctx = jnp.exp(s_ctx - m)
    den = (jnp.sum(p_loc, axis=-1, keepdims=True) + jnp.sum(p_ctx, axis=-1, keepdims=True)
           + jnp.exp(sk - m))
    o = (_dot(p_loc.astype(BF16), vl) + _dot(p_ctx.astype(BF16), vx_ref[...].astype(BF16))) / den
    for g in range(grp):
        o_ref[:, g * HEAD_DIM:(g + 1) * HEAD_DIM] = o[g * blk:(g + 1) * blk].astype(o_ref.dtype)


def swa_attention(proj, k_ctx, v_ctx, sink, cos2, sin2, *, row0, nseq, seq, q_col, k_col, v_col, n_kv, grp):
    blk = SWA_BLOCK
    nb = seq // blk
    rb0 = row0 // blk
    past = k_ctx.shape[2]
    qw = grp * HEAD_DIM
    qc, kc, vc = q_col // qw, k_col // HEAD_DIM, v_col // HEAD_DIM

    def kvspec(col, delta):
        return pl.BlockSpec((blk, HEAD_DIM),
                            lambda h, b, i: (rb0 + b * nb + jnp.clip(i + delta, 0, nb - 1), col + h))

    cache = pl.BlockSpec((None, None, past, HEAD_DIM), lambda h, b, i: (b, h, 0, 0))
    table = pl.BlockSpec((seq, HEAD_DIM), lambda h, b, i: (0, 0))
    kern = functools.partial(_swa_kernel, grp=grp, nb=nb)
    return pl.pallas_call(
        kern,
        grid=(n_kv, nseq, nb),
        in_specs=[pl.BlockSpec(memory_space=pltpu.SMEM),
                  pl.BlockSpec((blk, qw), lambda h, b, i: (rb0 + b * nb + i, qc + h)),
                  kvspec(kc, -1), kvspec(kc, 0), kvspec(kc, 1),
                  kvspec(vc, -1), kvspec(vc, 0), kvspec(vc, 1),
                  cache, cache, table, table],
        out_specs=pl.BlockSpec((blk, qw), lambda h, b, i: (b * nb + i, h)),
        out_shape=jax.ShapeDtypeStruct((nseq * seq, n_kv * qw), BF16),
        compiler_params=_params("parallel", "parallel", "parallel"),
        name="swa_attention",
    )(sink, proj, proj, proj, proj, proj, proj, proj, k_ctx, v_ctx, cos2, sin2)


def _conv_silu_kernel(x_ref, w_ref, b_ref, o_ref):
    x = x_ref[...]
    seq = x.shape[0]
    t = lax.broadcasted_iota(jnp.int32, (seq, 1), 0)

    def at(off):
        rolled = pltpu.roll(x, (seq - off) % seq, 0)
        return jnp.where((t + off >= 0) & (t + off < seq), rolled, 0.0)

    y = (w_ref[0:1, :] * at(-2) + w_ref[1:2, :] * at(-1) + w_ref[2:3, :] * x
         + w_ref[3:4, :] * at(1) + b_ref[...])
    o_ref[...] = _silu(y)


def conv_silu(proj, w, b, *, row0, nseq, seq, col0):
    width = w.shape[1]
    tc = 128
    rb0 = row0 // seq
    return pl.pallas_call(
        _conv_silu_kernel,
        grid=(nseq, width // tc),
        in_specs=[pl.BlockSpec((seq, tc), lambda b, j: (rb0 + b, col0 // tc + j)),
                  pl.BlockSpec((4, tc), lambda b, j: (0, j)),
                  pl.BlockSpec((1, tc), lambda b, j: (0, j))],
        out_specs=pl.BlockSpec((seq, tc), lambda b, j: (b, j)),
        out_shape=jax.ShapeDtypeStruct((nseq * seq, width), F32),
        compiler_params=_params("parallel", "parallel"),
        name="conv_silu",
    )(proj, w, b.reshape(1, width))


def _ssd_kernel(x_ref, b_ref, c_ref, dtc_ref, dtr_ref, plane_ref, psub_ref, h0_ref,
                y_ref, st_ref, ht, xw, *, rev, nchunks):
    g = pl.program_id(1)
    c = pl.program_id(2)
    q = SSD_CHUNK
    gh = SSD_GROUP_HEADS
    width = gh * SSD_HEADDIM
    hi = lax.Precision.HIGHEST

    @pl.when(c == 0)
    def _():
        ht[...] = h0_ref[...]

    dtc = pltpu.roll(dtc_ref[...], lax.rem((128 // gh - g) * gh, 128), 1)
    a_lane = -jnp.exp(plane_ref[0:1, :])
    a_sub = -jnp.exp(psub_ref[:, 0:1])
    dt_c = _softplus(dtc + plane_ref[1:2, :])
    dt_r = _softplus(dtr_ref[pl.ds(pl.multiple_of(g * gh, gh), gh), :] + psub_ref[:, 1:2])
    ii = lax.broadcasted_iota(jnp.int32, (q, q), 0)
    jj = lax.broadcasted_iota(jnp.int32, (q, q), 1)
    tri = (jj >= ii) if rev else (jj <= ii)
    tri_f = tri.astype(F32)
    acs_c = jnp.dot(tri_f, dt_c * a_lane, precision=hi, preferred_element_type=F32)
    acs_r = lax.dot_general(dt_r * a_sub, tri_f, NT_DIMS, precision=hi, preferred_element_type=F32)
    last = 0 if rev else q - 1
    acs_last = acs_c[last:last + 1, :]
    to_end = jnp.exp(acs_last - acs_c) * dt_c
    er = lax.broadcasted_iota(jnp.int32, (128, width), 0)
    ec = lax.broadcasted_iota(jnp.int32, (128, width), 1)
    expand = (ec // SSD_HEADDIM == er).astype(F32)
    to_end_w = jnp.dot(to_end, expand, precision=hi, preferred_element_type=F32)
    eac_w = jnp.dot(jnp.exp(acs_c), expand, precision=hi, preferred_element_type=F32)
    decay_w = jnp.dot(jnp.exp(jnp.broadcast_to(acs_last, (8, 128))), expand, precision=hi,
                      preferred_element_type=F32)[0:1, :]

    bm = b_ref[...].astype(BF16)
    cm = c_ref[...].astype(BF16)
    cb = _dot_nt(cm, bm)
    y_off = _dot(cm, ht[...].astype(BF16)) * eac_w
    lane = lax.broadcasted_iota(jnp.int32, (q, 2 * SSD_HEADDIM), 1)
    for p in range(gh // 2):
        cols = slice(p * 2 * SSD_HEADDIM, (p + 1) * 2 * SSD_HEADDIM)
        xp = x_ref[:, cols]
        yp = y_off[:, cols]
        for half in range(2):
            h = 2 * p + half
            seg = jnp.where(tri, acs_c[:, h:h + 1] - acs_r[h:h + 1, :], MASKED)
            w = cb * jnp.exp(seg) * dt_r[h:h + 1, :]
            in_half = (lane >= SSD_HEADDIM) if half else (lane < SSD_HEADDIM)
            yp = yp + _dot(w.astype(BF16), jnp.where(in_half, xp, 0.0).astype(BF16))
        y_ref[:, cols] = yp
    xw[...] = (x_ref[...] * to_end_w).astype(BF16)
    st = lax.dot_general(bm, xw[...], TN_DIMS, preferred_element_type=F32)
    ht[...] = ht[...] * decay_w + st

    @pl.when(c == nchunks - 1)
    def _():
        st_ref[...] = ht[...].T


def ssd_scan(xbc, dt_col, dt_row, plane, psub, h0t, *, row0, nseq, seq, rev, d_ssd, n_groups, d_state):
    q = SSD_CHUNK
    nchunks = seq // q
    width = SSD_GROUP_HEADS * SSD_HEADDIM
    rb0 = row0 // q
    bcol = d_ssd // d_state
    ccol = bcol + n_groups

    def chunk(c):
        return (nchunks - 1 - c) if rev else c

    kern = functools.partial(_ssd_kernel, rev=rev, nchunks=nchunks)
    return pl.pallas_call(
        kern,
        grid=(nseq, n_groups, nchunks),
        in_specs=[
            pl.BlockSpec((q, width), lambda b, g, c: (b * nchunks + chunk(c), g)),
            pl.BlockSpec((q, d_state), lambda b, g, c: (b * nchunks + chunk(c), bcol + g)),
            pl.BlockSpec((q, d_state), lambda b, g, c: (b * nchunks + chunk(c), ccol + g)),
            pl.BlockSpec((q, 128), lambda b, g, c: (rb0 + b * nchunks + chunk(c), 0)),
            pl.BlockSpec((n_groups * SSD_GROUP_HEADS, q), lambda b, g, c: (0, rb0 + b * nchunks + chunk(c))),
            pl.BlockSpec((None, 8, 128), lambda b, g, c: (g, 0, 0)),
            pl.BlockSpec((None, 8, 128), lambda b, g, c: (g, 0, 0)),
            pl.BlockSpec((None, None, d_state, width), lambda b, g, c: (b, g, 0, 0)),
        ],
        out_specs=[
            pl.BlockSpec((q, width), lambda b, g, c: (b * nchunks + chunk(c), g)),
            pl.BlockSpec((None, None, width, d_state), lambda b, g, c: (b, g, 0, 0)),
        ],
        out_shape=[jax.ShapeDtypeStruct((nseq * seq, d_ssd), F32),
                   jax.ShapeDtypeStruct((nseq, n_groups, width, d_state), F32)],
        scratch_shapes=[pltpu.VMEM((d_state, width), F32), pltpu.VMEM((q, width), BF16)],
        compiler_params=_params("parallel", "parallel", "arbitrary"),
        name="ssd_scan",
    )(xbc, xbc, xbc, dt_col, dt_row, plane, psub, h0t)


def _ssd_out_kernel(yf_ref, yb_ref, x_ref, d_ref, g_ref, *rest):
    z_refs, o_ref = rest[:-1], rest[-1]
    z = jnp.concatenate([r[...] for r in z_refs], axis=1)
    y = yf_ref[...] + yb_ref[...] + d_ref[...] * x_ref[...]
    o_ref[...] = _rms(y * _silu(z), g_ref[...]).astype(o_ref.dtype)


def ssd_out(yf, yb, xbc, proj, d_vec, norm_g, *, row0, z_col):
    m, d_ssd = yf.shape
    tm = EW_ROWS
    rb0 = row0 // tm
    zw = math.gcd(z_col, d_ssd)
    row = pl.BlockSpec((tm, d_ssd), lambda i: (i, 0))
    vec = pl.BlockSpec((1, d_ssd), lambda i: (0, 0))
    z_specs = [pl.BlockSpec((tm, zw), lambda i, p=p: (rb0 + i, z_col // zw + p)) for p in range(d_ssd // zw)]
    return pl.pallas_call(
        _ssd_out_kernel,
        grid=(m // tm,),
        in_specs=[row, row, row, vec, vec] + z_specs,
        out_specs=row,
        out_shape=jax.ShapeDtypeStruct((m, d_ssd), BF16),
        compiler_params=_params("parallel"),
        name="ssd_out",
    )(yf, yb, xbc, d_vec.reshape(1, d_ssd), norm_g.reshape(1, d_ssd), *([proj] * (d_ssd // zw)))


def ssd_params(a_log, dt_bias, n_groups):
    gh = SSD_GROUP_HEADS
    al = a_log.reshape(2, n_groups, gh)
    db = dt_bias.reshape(2, n_groups, gh)
    plane = jnp.zeros((2, n_groups, 8, 128), F32).at[:, :, 0, :gh].set(al).at[:, :, 1, :gh].set(db)
    psub = jnp.zeros((2, n_groups, 8, 128), F32).at[:, :, :, 0].set(al).at[:, :, :, 1].set(db)
    return plane, psub


def ssd_mixer(proj, dt_col, dt_row, h0, conv_w, conv_b, a_log, dt_bias, d_skip, norm_g, *,
              row0, nseq, seq, z_col, xbc_col, d_ssd, n_groups, d_state):
    width = SSD_GROUP_HEADS * SSD_HEADDIM
    xbc = conv_silu(proj, conv_w, conv_b, row0=row0, nseq=nseq, seq=seq, col0=xbc_col)
    plane, psub = ssd_params(a_log, dt_bias, n_groups)
    if h0 is None:
        h0t = jnp.zeros((2, nseq, n_groups, d_state, width), F32)
    else:
        h0t = h0.astype(F32).reshape(nseq, 2, n_groups, width, d_state).transpose(1, 0, 2, 4, 3)
    ys, sts = [], []
    for d in range(2):
        y, st = ssd_scan(xbc, dt_col, dt_row, plane[d], psub[d], h0t[d], row0=row0, nseq=nseq, seq=seq,
                         rev=bool(d), d_ssd=d_ssd, n_groups=n_groups, d_state=d_state)
        ys.append(y)
        sts.append(st.reshape(nseq, n_groups * SSD_GROUP_HEADS, SSD_HEADDIM, d_state))
    d_vec = jnp.repeat(d_skip.astype(F32), SSD_HEADDIM)
    y = ssd_out(ys[0], ys[1], xbc, proj, d_vec, norm_g, row0=row0, z_col=z_col)
    return y, jnp.stack(sts, axis=1)


def kernel(x_prompt, x_sample, cache_na_k, cache_na_v, state_rglru, cache_swa_k, cache_swa_v, state_ssd, c, c_ctx, w_mod, b_mod, norm_mix_pre, norm_mix_post, norm_ffn_pre, norm_ffn_post, w_in_even, rg_conv_w, rg_conv_b, rg_w_a, rg_b_a, rg_w_i, rg_b_i, rg_lambda, na_rel_bias, w_in_odd, swa_sink, ssd_conv_w, ssd_conv_b, ssd_a_log, ssd_dt_bias, ssd_d, ssd_norm, w_out, ffn_w_up, ffn_conv_w, ffn_conv_b, ffn_w_down):
    nb_ctx, ctx_len, d = x_prompt.shape
    nb_lat, lat_len, _ = x_sample.shape
    depth = w_mod.shape[0]
    n_ctx = nb_ctx * ctx_len
    n_lat = nb_lat * lat_len
    d_a = rg_conv_w.shape[2]
    n_heads_b = na_rel_bias.shape[1]
    d_b = n_heads_b * HEAD_DIM
    n_heads_c = swa_sink.shape[1]
    n_kv_c = cache_swa_k.shape[2]
    grp_c = n_heads_c // n_kv_c
    d_c = n_heads_c * HEAD_DIM
    d_kv = n_kv_c * HEAD_DIM
    n_ssd_heads = ssd_d.shape[1]
    d_ssd = n_ssd_heads * SSD_HEADDIM
    d_state = state_ssd.shape[-1]
    n_groups = n_ssd_heads // SSD_GROUP_HEADS
    d_xbc = ssd_conv_w.shape[2]
    d_ff = ffn_w_down.shape[1]
    d_ff_pad = -(-d_ff // 1024) * 1024
    shapes = dict(n_ctx=n_ctx, lat_len=lat_len)

    x = jnp.concatenate([x_prompt.reshape(n_ctx, d), x_sample.reshape(n_lat, d)], axis=0)
    c8 = jnp.zeros((MOD_ROWS, d), F32).at[0].set(c_ctx).at[1:1 + nb_lat].set(c)
    mod = modulation_all(c8, w_mod, b_mod)
    cos2, sin2 = rope_tables(lat_len)

    def pad_cols(a, n):
        return jnp.pad(a, ((0, 0), (0, n - a.shape[1])))

    new_na_k, new_na_v, new_rglru, new_swa_k, new_swa_v, new_ssd = [], [], [], [], [], []
    h = adaln(x, mod[0], norm_mix_pre[0], shift_i=0, scale_i=1, **shapes)
    for layer in range(depth):
        if layer % 2 == 0:
            e = layer // 2
            proj = matmul(h, w_in_even[e].astype(BF16))
            rg = (rg_conv_w[e], rg_conv_b[e], rg_w_a[e].astype(BF16), rg_b_a[e],
                  rg_w_i[e].astype(BF16), rg_b_i[e], rg_lambda[e])
            ya_c, st = rglru_mixer(proj, jnp.zeros((nb_ctx, 2, d_a), F32), *rg,
                                   row0=0, nseq=nb_ctx, seq=ctx_len, d_a=d_a, heads=4)
            ya_l, _ = rglru_mixer(proj, state_rglru[:, e].astype(F32), *rg,
                                  row0=n_ctx, nseq=nb_lat, seq=lat_len, d_a=d_a, heads=1)
            cols = dict(q_col=2 * d_a, k_col=2 * d_a + d_b, v_col=2 * d_a + 2 * d_b)
            yb_c, k_ctx, v_ctx = ctx_attention(proj, jnp.zeros((n_heads_b,), F32), nseq=nb_ctx, seq=ctx_len,
                                               n_kv=n_heads_b, grp=1, use_sink=False, **cols)
            yb_l = na_attention(proj, cache_na_k[:, e], cache_na_v[:, e], na_bias_table(na_rel_bias[e]),
                                row0=n_ctx, nseq=nb_lat, seq=lat_len, n_heads=n_heads_b, **cols)
            new_na_k.append(k_ctx)
            new_na_v.append(v_ctx)
            new_rglru.append(st)
            o = jnp.concatenate([jnp.concatenate([ya_c, yb_c], axis=1),
                                 jnp.concatenate([ya_l, yb_l], axis=1)], axis=0)
        else:
            od = layer // 2
            w_in = w_in_odd[od]
            n_main = d_c + 2 * d_kv + d_ssd + d_xbc
            proj = matmul(h, w_in[:, :n_main].astype(BF16))
            w_dt = w_in[:, n_main:].astype(BF16)
            dt_col, dt_row = dt_proj(h, pad_cols(w_dt, 128), w_dt.T)
            cols = dict(q_col=0, k_col=d_c, v_col=d_c + d_kv)
            yc_c, k_ctx, v_ctx = ctx_attention(proj, swa_sink[od], nseq=nb_ctx, seq=ctx_len,
                                               n_kv=n_kv_c, grp=grp_c, use_sink=True, **cols)
            yc_l = swa_attention(proj, cache_swa_k[:, od], cache_swa_v[:, od], swa_sink[od], cos2, sin2,
                                 row0=n_ctx, nseq=nb_lat, seq=lat_len, n_kv=n_kv_c, grp=grp_c, **cols)
            ssd = (ssd_conv_w[od], ssd_conv_b[od], ssd_a_log[od], ssd_dt_bias[od], ssd_d[od], ssd_norm[od])
            scols = dict(z_col=d_c + 2 * d_kv, xbc_col=d_c + 2 * d_kv + d_ssd, d_ssd=d_ssd,
                         n_groups=n_groups, d_state=d_state)
            yd_c, st = ssd_mixer(proj, dt_col, dt_row, None, *ssd, row0=0, nseq=nb_ctx, seq=ctx_len, **scols)
            yd_l, _ = ssd_mixer(proj, dt_col, dt_row, state_ssd[:, od], *ssd,
                                row0=n_ctx, nseq=nb_lat, seq=lat_len, **scols)
            new_swa_k.append(k_ctx)
            new_swa_v.append(v_ctx)
            new_ssd.append(st)
            o = jnp.concatenate([jnp.concatenate([yc_c, yd_c], axis=1),
                                 jnp.concatenate([yc_l, yd_l], axis=1)], axis=0)
        y = matmul(o, w_out[layer].astype(BF16))
        x, h = resid_adaln(x, y, mod[layer], norm_mix_post[layer], norm_ffn_pre[layer], mod[layer],
                           gate_i=2, shift_i=3, scale_i=4, **shapes)
        w_up = ffn_w_up[layer]
        cw = jnp.concatenate([ffn_conv_w[layer], ffn_conv_b[layer][None]], axis=0)
        cg = jnp.pad(cw[:, :d_ff], ((0, 4), (0, d_ff_pad - d_ff)))
        cv = jnp.pad(cw[:, d_ff:], ((0, 4), (0, d_ff_pad - d_ff)))
        t = ffn_up(h, pad_cols(w_up[:, :d_ff].astype(BF16), d_ff_pad),
                   pad_cols(w_up[:, d_ff:].astype(BF16), d_ff_pad), cg, cv,
                   n_ctx=n_ctx, ctx_len=ctx_len, lat_len=lat_len)
        w_down = jnp.pad(ffn_w_down[layer].astype(BF16), ((0, d_ff_pad - d_ff), (0, 0)))
        f = matmul_ktiled(t, w_down, tk=d_ff_pad // 4)
        nxt = min(layer + 1, depth - 1)
        x, h = resid_adaln(x, f, mod[layer], norm_ffn_post[layer], norm_mix_pre[nxt], mod[nxt],
                           gate_i=5, shift_i=0, scale_i=1, **shapes)
    return (x[:n_ctx].reshape(nb_ctx, ctx_len, d), x[n_ctx:].reshape(nb_lat, lat_len, d),
            jnp.stack(new_na_k, axis=1), jnp.stack(new_na_v, axis=1), jnp.stack(new_rglru, axis=1),
            jnp.stack(new_swa_k, axis=1), jnp.stack(new_swa_v, axis=1), jnp.stack(new_ssd, axis=1))
```

```python
import functools
import math

import jax
import jax.numpy as jnp
from jax import lax
from jax.experimental import pallas as pl
from jax.experimental.pallas import tpu as pltpu

F32 = jnp.float32
BF16 = jnp.bfloat16

RMS_EPS = 1e-6
HEAD_DIM = 128
GRID_W = 64
NA_ROWS = 8
NA_COLS = 16
NA_QROWS = 4
SWA_BLOCK = 128
SWA_WINDOW = 128
ROPE_BASE = 10000.0
RG_C = 8.0
SSD_CHUNK = 128
SSD_HEADDIM = 64
SSD_GROUP_HEADS = 8
MASKED = -1e30
MOD_ROWS = 8
V7X_VMEM_BYTES = 64 * 1024 * 1024
VMEM_LIMIT = V7X_VMEM_BYTES - 8 * 1024 * 1024
ROW_TILE = 1024
COL_TILE = 1024
EW_ROWS = 256
NT_DIMS = (((1,), (1,)), ((), ()))
TN_DIMS = (((0,), (0,)), ((), ()))


def _params(*sem):
    return pltpu.CompilerParams(dimension_semantics=sem, vmem_limit_bytes=VMEM_LIMIT)


def _silu(x):
    return x / (1.0 + jnp.exp(-x))


def _sigmoid(x):
    return 1.0 / (1.0 + jnp.exp(-x))


def _softplus(x):
    return jnp.maximum(x, 0.0) + jnp.log(1.0 + jnp.exp(-jnp.abs(x)))


def _gelu_tanh(x):
    return 0.5 * x * (1.0 + jnp.tanh(math.sqrt(2.0 / math.pi) * (x + 0.044715 * (x * x * x))))


def _rms(x, g):
    ms = jnp.mean(x * x, axis=-1, keepdims=True)
    return x * lax.rsqrt(ms + RMS_EPS) * g


def _dot(a, b):
    return jnp.dot(a, b, preferred_element_type=F32)


def _dot_nt(a, b):
    return lax.dot_general(a, b, NT_DIMS, preferred_element_type=F32)


def _mod_kernel(c_ref, w_ref, b_ref, o_ref):
    s = _silu(c_ref[...])
    o_ref[...] = _dot(s.astype(BF16), w_ref[...].astype(BF16)) + b_ref[...]


def modulation_all(c8, w_mod, b_mod):
    depth, d, n6 = w_mod.shape
    tn = 512
    per = d // tn
    return pl.pallas_call(
        _mod_kernel,
        grid=(depth, n6 // tn),
        in_specs=[
            pl.BlockSpec((MOD_ROWS, d), lambda l, j: (0, 0)),
            pl.BlockSpec((None, d, tn), lambda l, j: (l, 0, j)),
            pl.BlockSpec((None, 1, tn), lambda l, j: (l, 0, j)),
        ],
        out_specs=pl.BlockSpec((None, None, MOD_ROWS, tn), lambda l, j: (l, j // per, 0, j % per)),
        out_shape=jax.ShapeDtypeStruct((depth, 6, MOD_ROWS, d), F32),
        compiler_params=_params("parallel", "parallel"),
        name="modulation",
    )(c8, w_mod, b_mod.reshape(depth, 1, n6))


def _mod_row(i, tm, n_ctx, lat_len):
    row0 = i * tm
    return jnp.where(row0 < n_ctx, 0, 1 + (row0 - n_ctx) // lat_len)


def _adaln_kernel(x_ref, mod_ref, g_ref, h_ref, *, tm, n_ctx, lat_len, shift_i, scale_i):
    r = _mod_row(pl.program_id(0), tm, n_ctx, lat_len)
    shift = mod_ref[shift_i, pl.ds(r, 1), :]
    scale = mod_ref[scale_i, pl.ds(r, 1), :]
    h_ref[...] = (_rms(x_ref[...], g_ref[...]) * (1.0 + scale) + shift).astype(h_ref.dtype)


def adaln(x, mod, g, *, n_ctx, lat_len, shift_i, scale_i):
    m, d = x.shape
    tm = EW_ROWS
    kern = functools.partial(_adaln_kernel, tm=tm, n_ctx=n_ctx, lat_len=lat_len,
                             shift_i=shift_i, scale_i=scale_i)
    return pl.pallas_call(
        kern,
        grid=(m // tm,),
        in_specs=[
            pl.BlockSpec((tm, d), lambda i: (i, 0)),
            pl.BlockSpec((6, MOD_ROWS, d), lambda i: (0, 0, 0)),
            pl.BlockSpec((1, d), lambda i: (0, 0)),
        ],
        out_specs=pl.BlockSpec((tm, d), lambda i: (i, 0)),
        out_shape=jax.ShapeDtypeStruct((m, d), BF16),
        compiler_params=_params("parallel"),
        name="adaln",
    )(x, mod, g.reshape(1, d))


def _resid_kernel(x_ref, y_ref, mod_ref, gpost_ref, gpre_ref, modn_ref, xo_ref, h_ref, *,
                  tm, n_ctx, lat_len, gate_i, shift_i, scale_i):
    r = _mod_row(pl.program_id(0), tm, n_ctx, lat_len)
    gate = mod_ref[gate_i, pl.ds(r, 1), :]
    x = x_ref[...] + gate * _rms(y_ref[...], gpost_ref[...])
    xo_ref[...] = x
    shift = modn_ref[shift_i, pl.ds(r, 1), :]
    scale = modn_ref[scale_i, pl.ds(r, 1), :]
    h_ref[...] = (_rms(x, gpre_ref[...]) * (1.0 + scale) + shift).astype(h_ref.dtype)


def resid_adaln(x, y, mod, gpost, gpre, modn, *, n_ctx, lat_len, gate_i, shift_i, scale_i):
    m, d = x.shape
    tm = EW_ROWS
    kern = functools.partial(_resid_kernel, tm=tm, n_ctx=n_ctx, lat_len=lat_len,
                             gate_i=gate_i, shift_i=shift_i, scale_i=scale_i)
    row = pl.BlockSpec((tm, d), lambda i: (i, 0))
    vec = pl.BlockSpec((1, d), lambda i: (0, 0))
    modspec = pl.BlockSpec((6, MOD_ROWS, d), lambda i: (0, 0, 0))
    return pl.pallas_call(
        kern,
        grid=(m // tm,),
        in_specs=[row, row, modspec, vec, vec, modspec],
        out_specs=[row, row],
        out_shape=[jax.ShapeDtypeStruct((m, d), F32), jax.ShapeDtypeStruct((m, d), BF16)],
        compiler_params=_params("parallel"),
        name="resid_adaln",
    )(x, y, mod, gpost.reshape(1, d), gpre.reshape(1, d), modn)


def _mm_kernel(a_ref, w_ref, o_ref):
    o_ref[...] = _dot(a_ref[...], w_ref[...]).astype(o_ref.dtype)


def matmul(a, w, out_dtype=F32):
    m, k = a.shape
    n = w.shape[1]
    tm, tn = min(ROW_TILE, m), min(COL_TILE, n)
    return pl.pallas_call(
        _mm_kernel,
        grid=(m // tm, n // tn),
        in_specs=[pl.BlockSpec((tm, k), lambda i, j: (i, 0)),
                  pl.BlockSpec((k, tn), lambda i, j: (0, j))],
        out_specs=pl.BlockSpec((tm, tn), lambda i, j: (i, j)),
        out_shape=jax.ShapeDtypeStruct((m, n), out_dtype),
        compiler_params=_params("parallel", "parallel"),
        name="matmul",
    )(a, w)


def _mm_acc_kernel(a_ref, w_ref, o_ref):
    @pl.when(pl.program_id(2) == 0)
    def _():
        o_ref[...] = jnp.zeros_like(o_ref)

    o_ref[...] += _dot(a_ref[...], w_ref[...])


def matmul_ktiled(a, w, tk):
    m, k = a.shape
    n = w.shape[1]
    tm, tn = min(ROW_TILE, m), min(COL_TILE, n)
    return pl.pallas_call(
        _mm_acc_kernel,
        grid=(m // tm, n // tn, k // tk),
        in_specs=[pl.BlockSpec((tm, tk), lambda i, j, kk: (i, kk)),
                  pl.BlockSpec((tk, tn), lambda i, j, kk: (kk, j))],
        out_specs=pl.BlockSpec((tm, tn), lambda i, j, kk: (i, j)),
        out_shape=jax.ShapeDtypeStruct((m, n), F32),
        compiler_params=_params("parallel", "parallel", "arbitrary"),
        name="matmul_ktiled",
    )(a, w)


def _dt_kernel(h_ref, w_ref, wt_ref, oc_ref, or_ref):
    h = h_ref[...]
    oc_ref[...] = _dot(h, w_ref[...])
    or_ref[...] = _dot_nt(wt_ref[...], h)


def dt_proj(h, w_pad, w_t):
    m, k = h.shape
    nh = w_t.shape[0]
    tm = min(ROW_TILE, m)
    return pl.pallas_call(
        _dt_kernel,
        grid=(m // tm,),
        in_specs=[pl.BlockSpec((tm, k), lambda i: (i, 0)),
                  pl.BlockSpec((k, 128), lambda i: (0, 0)),
                  pl.BlockSpec((nh, k), lambda i: (0, 0))],
        out_specs=[pl.BlockSpec((tm, 128), lambda i: (i, 0)),
                   pl.BlockSpec((nh, tm), lambda i: (0, i))],
        out_shape=[jax.ShapeDtypeStruct((m, 128), F32), jax.ShapeDtypeStruct((nh, m), F32)],
        compiler_params=_params("parallel"),
        name="dt_proj",
    )(h, w_pad, w_t)


FFN_HALO = 16


def _ffn_up_kernel(a_ref, ap_ref, an_ref, wg_ref, wv_ref, cg_ref, cv_ref, o_ref, aext, *,
                   tm, n_ctx, ctx_len, lat_len):
    i = pl.program_id(0)

    @pl.when(pl.program_id(1) == 0)
    def _():
        aext[0:FFN_HALO, :] = ap_ref[...]
        aext[FFN_HALO:FFN_HALO + tm, :] = a_ref[...]
        aext[FFN_HALO + tm:, :] = an_ref[...]

    a = aext[...]
    g = _dot(a, wg_ref[...])
    v = _dot(a, wv_ref[...])
    local = lax.broadcasted_iota(jnp.int32, (tm, 1), 0)
    is_ctx = i * tm < n_ctx
    pos = jnp.where(is_ctx, lax.rem(local, ctx_len), lax.rem(i * tm - n_ctx, lat_len) + local)
    has_prev = pos != 0
    has_next = pos != jnp.where(is_ctx, ctx_len - 1, lat_len - 1)
    rows = tm + 2 * FFN_HALO

    def conv(u, c_ref):
        up = pltpu.roll(u, 1, 0)[FFN_HALO:FFN_HALO + tm]
        un = pltpu.roll(u, rows - 1, 0)[FFN_HALO:FFN_HALO + tm]
        uc = u[FFN_HALO:FFN_HALO + tm]
        return (c_ref[0:1, :] * jnp.where(has_prev, up, 0.0) + c_ref[1:2, :] * uc
                + c_ref[2:3, :] * jnp.where(has_next, un, 0.0) + c_ref[3:4, :])

    o_ref[...] = (_silu(conv(g, cg_ref)) * conv(v, cv_ref)).astype(o_ref.dtype)


def ffn_up(h, wg, wv, cg, cv, *, n_ctx, ctx_len, lat_len):
    m, k = h.shape
    n = wg.shape[1]
    tm = min(ROW_TILE, m)
    tn = 512
    assert tm % ctx_len == 0 and lat_len % tm == 0 and n_ctx % tm == 0
    nhalo = m // FFN_HALO
    per = tm // FFN_HALO
    kern = functools.partial(_ffn_up_kernel, tm=tm, n_ctx=n_ctx, ctx_len=ctx_len, lat_len=lat_len)
    return pl.pallas_call(
        kern,
        grid=(m // tm, n // tn),
        in_specs=[
            pl.BlockSpec((tm, k), lambda i, j: (i, 0)),
            pl.BlockSpec((FFN_HALO, k), lambda i, j: (jnp.maximum(i * per - 1, 0), 0)),
            pl.BlockSpec((FFN_HALO, k), lambda i, j: (jnp.minimum((i + 1) * per, nhalo - 1), 0)),
            pl.BlockSpec((k, tn), lambda i, j: (0, j)),
            pl.BlockSpec((k, tn), lambda i, j: (0, j)),
            pl.BlockSpec((8, tn), lambda i, j: (0, j)),
            pl.BlockSpec((8, tn), lambda i, j: (0, j)),
        ],
        out_specs=pl.BlockSpec((tm, tn), lambda i, j: (i, j)),
        out_shape=jax.ShapeDtypeStruct((m, n), BF16),
        scratch_shapes=[pltpu.VMEM((tm + 2 * FFN_HALO, k), BF16)],
        compiler_params=_params("parallel", "arbitrary"),
        name="ffn_up",
    )(h, h, h, wg, wv, cg, cv)


RG_CHUNK = 256
RG_PAD = 8


def _scan_chunk(a, u, h, reverse):
    nblk = a.shape[0] // 8
    row8 = lax.broadcasted_iota(jnp.int32, (8, a.shape[1]), 0)
    outs = [None] * nblk
    for b in (range(nblk - 1, -1, -1) if reverse else range(nblk)):
        aa = a[8 * b:8 * b + 8]
        uu = u[8 * b:8 * b + 8]
        for s in (1, 2, 4):
            keep = (row8 < 8 - s) if reverse else (row8 >= s)
            shift = 8 - s if reverse else s
            a_sh = jnp.where(keep, pltpu.roll(aa, shift, 0), 1.0)
            u_sh = jnp.where(keep, pltpu.roll(uu, shift, 0), 0.0)
            uu = aa * u_sh + uu
            aa = aa * a_sh
        hb = aa * h + uu
        outs[b] = hb
        h = hb[0:1] if reverse else hb[7:8]
    return jnp.concatenate(outs, axis=0), h


def _rglru_kernel(ga_ref, xa_ref, h0_ref, cw_ref, cb_ref, wa_ref, ba_ref, wi_ref, bi_ref, lam_ref,
                  y_ref, st_ref, xpad, hf, *, seq, heads):
    chunk = min(RG_CHUNK, seq)
    nchunks = seq // chunk
    width = heads * HEAD_DIM
    xpad[0:RG_PAD, :] = jnp.zeros((RG_PAD, width), F32)
    xpad[RG_PAD + seq:, :] = jnp.zeros((RG_PAD, width), F32)
    xpad[RG_PAD:RG_PAD + seq, :] = xa_ref[...]
    win_rows = chunk + 2 * RG_PAD

    for hh in range(heads):
        lanes = slice(hh * HEAD_DIM, (hh + 1) * HEAD_DIM)
        rate = [(-RG_C) * _softplus(-lam_ref[d:d + 1, lanes]) for d in range(2)]

        def coeffs(t0, d, lanes=lanes, hh=hh, rate=rate):
            win = xpad[pl.ds(t0, win_rows), lanes]

            def at(off):
                return pltpu.roll(win, (win_rows - off) % win_rows, 0)[RG_PAD:RG_PAD + chunk]

            xc = (cw_ref[0:1, lanes] * at(-2) + cw_ref[1:2, lanes] * at(-1)
                  + cw_ref[2:3, lanes] * at(0) + cw_ref[3:4, lanes] * at(1) + cb_ref[0:1, lanes])
            xb = xc.astype(BF16)
            r = _sigmoid(_dot(xb, wa_ref[d, hh]) + ba_ref[d:d + 1, lanes])
            gate_i = _sigmoid(_dot(xb, wi_ref[d, hh]) + bi_ref[d:d + 1, lanes])
            a = jnp.exp(r * rate[d])
            u = jnp.sqrt(1.0 - a * a) * (gate_i * xc)
            return a, u

        def fwd(c, h, lanes=lanes, coeffs=coeffs):
            t0 = pl.multiple_of(c * chunk, chunk)
            a, u = coeffs(t0, 0)
            hs, h = _scan_chunk(a, u, h, reverse=False)
            hf[pl.ds(t0, chunk), lanes] = hs
            return h

        h_f = lax.fori_loop(0, nchunks, fwd, h0_ref[0:1, lanes])

        def bwd(cc, h, lanes=lanes, coeffs=coeffs):
            t0 = pl.multiple_of((nchunks - 1 - cc) * chunk, chunk)
            a, u = coeffs(t0, 1)
            hs, h = _scan_chunk(a, u, h, reverse=True)
            y = (hf[pl.ds(t0, chunk), lanes] + hs) * _gelu_tanh(ga_ref[pl.ds(t0, chunk), lanes])
            y_ref[pl.ds(t0, chunk), lanes] = y.astype(y_ref.dtype)
            return h

        h_b = lax.fori_loop(0, nchunks, bwd, h0_ref[1:2, lanes])
        st_ref[0:1, lanes] = h_f
        st_ref[1:2, lanes] = h_b


def rglru_mixer(proj, h0, conv_w, conv_b, w_a, b_a, w_i, b_i, lam, *, row0, nseq, seq, d_a, heads):
    width = heads * HEAD_DIM
    ncol = d_a // width
    rb0 = row0 // seq
    kern = functools.partial(_rglru_kernel, seq=seq, heads=heads)
    vec2 = pl.BlockSpec((2, width), lambda b, j: (0, j))
    return pl.pallas_call(
        kern,
        grid=(nseq, ncol),
        in_specs=[
            pl.BlockSpec((seq, width), lambda b, j: (rb0 + b, j)),
            pl.BlockSpec((seq, width), lambda b, j: (rb0 + b, ncol + j)),
            pl.BlockSpec((None, 2, width), lambda b, j: (b, 0, j)),
            pl.BlockSpec((4, width), lambda b, j: (0, j)),
            pl.BlockSpec((1, width), lambda b, j: (0, j)),
            pl.BlockSpec((2, heads, HEAD_DIM, HEAD_DIM), lambda b, j: (0, j, 0, 0)),
            vec2,
            pl.BlockSpec((2, heads, HEAD_DIM, HEAD_DIM), lambda b, j: (0, j, 0, 0)),
            vec2,
            vec2,
        ],
        out_specs=[pl.BlockSpec((seq, width), lambda b, j: (b, j)),
                   pl.BlockSpec((None, 2, width), lambda b, j: (b, 0, j))],
        out_shape=[jax.ShapeDtypeStruct((nseq * seq, d_a), BF16),
                   jax.ShapeDtypeStruct((nseq, 2, d_a), F32)],
        scratch_shapes=[pltpu.VMEM((seq + 2 * RG_PAD, width), F32), pltpu.VMEM((seq, width), F32)],
        compiler_params=_params("parallel", "parallel"),
        name="rglru",
    )(proj, proj, h0, conv_w, conv_b.reshape(1, d_a), w_a, b_a, w_i, b_i, lam)


def _sink_column(sink_ref, first_head, grp, rows_per_head):
    rows = lax.broadcasted_iota(jnp.int32, (grp * rows_per_head, 1), 0)
    col = jnp.zeros((grp * rows_per_head, 1), F32)
    for g in range(grp):
        in_g = (rows >= g * rows_per_head) & (rows < (g + 1) * rows_per_head)
        col = jnp.where(in_g, sink_ref[first_head + g], col)
    return col


def _ctx_attn_kernel(sink_ref, q_ref, k_ref, v_ref, o_ref, ko_ref, vo_ref, *, nkv, grp, use_sink):
    seq = q_ref.shape[0]
    scale = HEAD_DIM ** -0.5
    for kk in range(nkv):
        k = k_ref[:, kk * HEAD_DIM:(kk + 1) * HEAD_DIM]
        v = v_ref[:, kk * HEAD_DIM:(kk + 1) * HEAD_DIM]
        ko_ref[kk] = k
        vo_ref[kk] = v
        qs = jnp.concatenate(
            [q_ref[:, (kk * grp + g) * HEAD_DIM:(kk * grp + g + 1) * HEAD_DIM] for g in range(grp)], axis=0)
        s = _dot_nt((qs * scale).astype(BF16), k.astype(BF16))
        m = jnp.max(s, axis=-1, keepdims=True)
        if use_sink:
            first = (pl.program_id(1) * nkv + kk) * grp
            sk = _sink_column(sink_ref, first, grp, seq)
            m = jnp.maximum(m, sk)
        p = jnp.exp(s - m)
        den = jnp.sum(p, axis=-1, keepdims=True)
        if use_sink:
            den = den + jnp.exp(sk - m)
        o = _dot(p.astype(BF16), v.astype(BF16)) / den
        for g in range(grp):
            hq = kk * grp + g
            o_ref[:, hq * HEAD_DIM:(hq + 1) * HEAD_DIM] = o[g * seq:(g + 1) * seq].astype(o_ref.dtype)


def ctx_attention(proj, sink, *, nseq, seq, q_col, k_col, v_col, n_kv, grp, use_sink):
    nkv = max(1, 4 // grp)
    qw = nkv * grp * HEAD_DIM
    kw = nkv * HEAD_DIM
    kern = functools.partial(_ctx_attn_kernel, nkv=nkv, grp=grp, use_sink=use_sink)
    return pl.pallas_call(
        kern,
        grid=(nseq, n_kv // nkv),
        in_specs=[
            pl.BlockSpec(memory_space=pltpu.SMEM),
            pl.BlockSpec((seq, qw), lambda b, j: (b, q_col // qw + j)),
            pl.BlockSpec((seq, kw), lambda b, j: (b, k_col // kw + j)),
            pl.BlockSpec((seq, kw), lambda b, j: (b, v_col // kw + j)),
        ],
        out_specs=[
            pl.BlockSpec((seq, qw), lambda b, j: (b, j)),
            pl.BlockSpec((None, nkv, seq, HEAD_DIM), lambda b, j: (b, j, 0, 0)),
            pl.BlockSpec((None, nkv, seq, HEAD_DIM), lambda b, j: (b, j, 0, 0)),
        ],
        out_shape=[
            jax.ShapeDtypeStruct((nseq * seq, n_kv * grp * HEAD_DIM), BF16),
            jax.ShapeDtypeStruct((nseq, n_kv, seq, HEAD_DIM), F32),
            jax.ShapeDtypeStruct((nseq, n_kv, seq, HEAD_DIM), F32),
        ],
        compiler_params=_params("parallel", "parallel"),
        name="ctx_attention",
    )(sink, proj, proj, proj)


def na_bias_table(rel_bias):
    nh = rel_bias.shape[0]
    nk = 3 * NA_QROWS
    padded = jnp.pad(rel_bias.astype(F32), ((0, 0), (0, 0), (GRID_W - NA_COLS, GRID_W - NA_COLS)))
    toep = jnp.stack([padded[:, :, GRID_W - 1 - c:2 * GRID_W - 1 - c] for c in range(GRID_W)], axis=2)
    c = jnp.arange(GRID_W)[:, None]
    j = jnp.arange(GRID_W)[None, :]
    cs = jnp.clip(c - NA_COLS // 2, 0, GRID_W - NA_COLS)
    toep = jnp.where((j >= cs) & (j < cs + NA_COLS), toep, MASKED)
    d0 = NA_ROWS - 1 - NA_QROWS
    per_q = jnp.stack([toep[:, d0 - qi:d0 - qi + nk].transpose(0, 2, 1, 3) for qi in range(NA_QROWS)],
                      axis=1)
    qi = jnp.arange(NA_QROWS)[:, None]
    ki = jnp.arange(nk)[None, :]
    row_ok = jnp.stack([
        (ki >= NA_QROWS) & (ki < NA_QROWS + NA_ROWS) & (qi >= 0),
        (ki >= qi) & (ki < qi + NA_ROWS),
        (ki >= 0) & (ki < NA_ROWS) & (qi >= 0),
    ])
    tab = jnp.where(row_ok[:, None, :, None, :, None], per_q[None], MASKED)
    return tab.reshape(3, nh, NA_QROWS * GRID_W, nk * GRID_W)


NA_STEP_HEADS = 2


def _na_kernel(q_ref, kp_ref, kc_ref, kn_ref, vp_ref, vc_ref, vn_ref, kx_ref, vx_ref, bias_ref, o_ref):
    for hh in range(NA_STEP_HEADS):
        cols = slice(hh * HEAD_DIM, (hh + 1) * HEAD_DIM)
        q = (q_ref[:, cols] * HEAD_DIM ** -0.5).astype(BF16)
        kl = jnp.concatenate([kp_ref[:, cols], kc_ref[:, cols], kn_ref[:, cols]], axis=0).astype(BF16)
        vl = jnp.concatenate([vp_ref[:, cols], vc_ref[:, cols], vn_ref[:, cols]], axis=0).astype(BF16)
        s_loc = _dot_nt(q, kl) + bias_ref[hh]
        s_ctx = _dot_nt(q, kx_ref[hh].astype(BF16))
        m = jnp.maximum(jnp.max(s_loc, axis=-1, keepdims=True), jnp.max(s_ctx, axis=-1, keepdims=True))
        p_loc = jnp.exp(s_loc - m)
        p_ctx = jnp.exp(s_ctx - m)
        den = jnp.sum(p_loc, axis=-1, keepdims=True) + jnp.sum(p_ctx, axis=-1, keepdims=True)
        o = _dot(p_loc.astype(BF16), vl) + _dot(p_ctx.astype(BF16), vx_ref[hh].astype(BF16))
        o_ref[:, cols] = (o / den).astype(o_ref.dtype)


def na_attention(proj, k_ctx, v_ctx, bias_tab, *, row0, nseq, seq, q_col, k_col, v_col, n_heads):
    blk = NA_QROWS * GRID_W
    nb = seq // blk
    rb0 = row0 // blk
    past = k_ctx.shape[2]
    hs = NA_STEP_HEADS
    width = hs * HEAD_DIM
    qc, kc, vc = q_col // width, k_col // width, v_col // width

    def rows(b, i):
        return rb0 + b * nb + i

    def spec(col, delta):
        return pl.BlockSpec((blk, width),
                            lambda h, b, i: (rows(b, jnp.clip(i + delta, 0, nb - 1)), col + h))

    def block_type(i):
        return jnp.where(i == 0, 0, jnp.where(i == nb - 1, 2, 1))

    cache = pl.BlockSpec((None, hs, past, HEAD_DIM), lambda h, b, i: (b, h, 0, 0))
    return pl.pallas_call(
        _na_kernel,
        grid=(n_heads // hs, nseq, nb),
        in_specs=[spec(qc, 0), spec(kc, -1), spec(kc, 0), spec(kc, 1),
                  spec(vc, -1), spec(vc, 0), spec(vc, 1), cache, cache,
                  pl.BlockSpec((None, hs, blk, 3 * blk), lambda h, b, i: (block_type(i), h, 0, 0))],
        out_specs=pl.BlockSpec((blk, width), lambda h, b, i: (b * nb + i, h)),
        out_shape=jax.ShapeDtypeStruct((nseq * seq, n_heads * HEAD_DIM), BF16),
        compiler_params=_params("parallel", "parallel", "parallel"),
        name="na_attention",
    )(proj, proj, proj, proj, proj, proj, proj, k_ctx, v_ctx, bias_tab)


def rope_tables(seq):
    t = jnp.arange(seq)
    row = (t // GRID_W).astype(F32)
    col = (t % GRID_W).astype(F32)
    n = HEAD_DIM // 4
    inv = ROPE_BASE ** (-jnp.arange(n, dtype=F32) / n)
    ang = jnp.concatenate([row[:, None] * inv, col[:, None] * inv], axis=-1)
    cos, sin = jnp.cos(ang), jnp.sin(ang)
    return jnp.concatenate([cos, cos], axis=-1), jnp.concatenate([-sin, sin], axis=-1)


SWA_STEP_KV = 2


def _rope(x, cos, sin):
    return x * cos + pltpu.roll(x, HEAD_DIM // 2, 1) * sin


def _swa_kernel(sink_ref, q_ref, kp_ref, kc_ref, kn_ref, vp_ref, vc_ref, vn_ref, kx_ref, vx_ref,
                cos_ref, sin_ref, o_ref, *, grp, nb):
    kvh = pl.program_id(0)
    i = pl.program_id(2)
    blk = SWA_BLOCK

    def tables(ib):
        t0 = pl.multiple_of(ib * blk, blk)
        return cos_ref[pl.ds(t0, blk), :], sin_ref[pl.ds(t0, blk), :]

    cq, sq = tables(i)
    cp, sp = tables(jnp.maximum(i - 1, 0))
    cn, sn = tables(jnp.minimum(i + 1, nb - 1))
    scale = HEAD_DIM ** -0.5
    qi = lax.broadcasted_iota(jnp.int32, (blk, 3 * blk), 0)
    kj = lax.broadcasted_iota(jnp.int32, (blk, 3 * blk), 1)
    ok = (jnp.abs(kj - blk - qi) <= SWA_WINDOW)
    ok = ok & ((kj >= blk) | (i > 0)) & ((kj < 2 * blk) | (i < nb - 1))
    okg = jnp.concatenate([ok] * grp, axis=0)
    for kk in range(SWA_STEP_KV):
        kcols = slice(kk * HEAD_DIM, (kk + 1) * HEAD_DIM)
        q0 = kk * grp
        qs = jnp.concatenate(
            [_rope(q_ref[:, (q0 + g) * HEAD_DIM:(q0 + g + 1) * HEAD_DIM], cq, sq) * scale for g in range(grp)],
            axis=0).astype(BF16)
        kl = jnp.concatenate([_rope(kp_ref[:, kcols], cp, sp), _rope(kc_ref[:, kcols], cq, sq),
                              _rope(kn_ref[:, kcols], cn, sn)], axis=0).astype(BF16)
        vl = jnp.concatenate([vp_ref[:, kcols], vc_ref[:, kcols], vn_ref[:, kcols]], axis=0).astype(BF16)
        s_loc = jnp.where(okg, _dot_nt(qs, kl), MASKED)
        s_ctx = _dot_nt(qs, kx_ref[kk].astype(BF16))
        sk = _sink_column(sink_ref, (kvh * SWA_STEP_KV + kk) * grp, grp, blk)
        m = jnp.maximum(jnp.maximum(jnp.max(s_loc, axis=-1, keepdims=True),
                                    jnp.max(s_ctx, axis=-1, keepdims=True)), sk)
        p_loc = jnp.exp(s_loc - m)
        p_ctx = jnp.exp(s_ctx - m)
        den = (jnp.sum(p_loc, axis=-1, keepdims=True) + jnp.sum(p_ctx, axis=-1, keepdims=True)
               + jnp.exp(sk - m))
        o = (_dot(p_loc.astype(BF16), vl) + _dot(p_ctx.astype(BF16), vx_ref[kk].astype(BF16))) / den
        for g in range(grp):
            o_ref[:, (q0 + g) * HEAD_DIM:(q0 + g + 1) * HEAD_DIM] = o[g * blk:(g + 1) * blk].astype(o_ref.dtype)


def swa_attention(proj, k_ctx, v_ctx, sink, cos2, sin2, *, row0, nseq, seq, q_col, k_col, v_col, n_kv, grp):
    blk = SWA_BLOCK
    nb = seq // blk
    rb0 = row0 // blk
    past = k_ctx.shape[2]
    kw = SWA_STEP_KV * HEAD_DIM
    qw = grp * kw
    qc, kc, vc = q_col // qw, k_col // kw, v_col // kw

    def kvspec(col, delta):
        return pl.BlockSpec((blk, kw),
                            lambda h, b, i: (rb0 + b * nb + jnp.clip(i + delta, 0, nb - 1), col + h))

    cache = pl.BlockSpec((None, SWA_STEP_KV, past, HEAD_DIM), lambda h, b, i: (b, h, 0, 0))
    table = pl.BlockSpec((seq, HEAD_DIM), lambda h, b, i: (0, 0))
    kern = functools.partial(_swa_kernel, grp=grp, nb=nb)
    return pl.pallas_call(
        kern,
        grid=(n_kv // SWA_STEP_KV, nseq, nb),
        in_specs=[pl.BlockSpec(memory_space=pltpu.SMEM),
                  pl.BlockSpec((blk, qw), lambda h, b, i: (rb0 + b * nb + i, qc + h)),
                  kvspec(kc, -1), kvspec(kc, 0), kvspec(kc, 1),
                  kvspec(vc, -1), kvspec(vc, 0), kvspec(vc, 1),
                  cache, cache, table, table],
        out_specs=pl.BlockSpec((blk, qw), lambda h, b, i: (b * nb + i, h)),
        out_shape=jax.ShapeDtypeStruct((nseq * seq, n_kv * grp * HEAD_DIM), BF16),
        compiler_params=_params("parallel", "parallel", "parallel"),
        name="swa_attention",
    )(sink, proj, proj, proj, proj, proj, proj, proj, k_ctx, v_ctx, cos2, sin2)


CONV_CHUNK = 256
CONV_PAD = 8


def _conv_silu_kernel(x_ref, w_ref, b_ref, o_ref, xpad, *, seq):
    chunk = min(CONV_CHUNK, seq)
    width = x_ref.shape[1]
    xpad[0:CONV_PAD, :] = jnp.zeros((CONV_PAD, width), F32)
    xpad[CONV_PAD + seq:, :] = jnp.zeros((CONV_PAD, width), F32)
    xpad[CONV_PAD:CONV_PAD + seq, :] = x_ref[...]
    win_rows = chunk + 2 * CONV_PAD

    def body(c, carry):
        t0 = pl.multiple_of(c * chunk, chunk)
        win = xpad[pl.ds(t0, win_rows), :]

        def at(off):
            return pltpu.roll(win, (win_rows - off) % win_rows, 0)[CONV_PAD:CONV_PAD + chunk]

        y = (w_ref[0:1, :] * at(-2) + w_ref[1:2, :] * at(-1) + w_ref[2:3, :] * at(0)
             + w_ref[3:4, :] * at(1) + b_ref[...])
        o_ref[pl.ds(t0, chunk), :] = _silu(y)
        return carry

    lax.fori_loop(0, seq // chunk, body, 0)


def conv_silu(proj, w, b, *, row0, nseq, seq, col0):
    width = w.shape[1]
    tc = 256
    rb0 = row0 // seq
    return pl.pallas_call(
        functools.partial(_conv_silu_kernel, seq=seq),
        grid=(nseq, width // tc),
        in_specs=[pl.BlockSpec((seq, tc), lambda b, j: (rb0 + b, col0 // tc + j)),
                  pl.BlockSpec((4, tc), lambda b, j: (0, j)),
                  pl.BlockSpec((1, tc), lambda b, j: (0, j))],
        out_specs=pl.BlockSpec((seq, tc), lambda b, j: (b, j)),
        out_shape=jax.ShapeDtypeStruct((nseq * seq, width), F32),
        scratch_shapes=[pltpu.VMEM((seq + 2 * CONV_PAD, tc), F32)],
        compiler_params=_params("parallel", "parallel"),
        name="conv_silu",
    )(proj, w, b.reshape(1, width))


SSD_STEP_GROUPS = 4


def _ssd_kernel(x_ref, b_ref, c_ref, dtc_ref, dtr_ref, plane_ref, psub_ref, h0_ref,
                y_ref, st_ref, ht, xw, *, rev, nchunks, n_state):
    c = pl.program_id(2)
    q = SSD_CHUNK
    gh = SSD_GROUP_HEADS
    gw = gh * SSD_HEADDIM
    hi = lax.Precision.HIGHEST

    @pl.when(c == 0)
    def _():
        ht[...] = h0_ref[...]

    ii = lax.broadcasted_iota(jnp.int32, (q, q), 0)
    jj = lax.broadcasted_iota(jnp.int32, (q, q), 1)
    tri = (jj >= ii) if rev else (jj <= ii)
    tri_f = tri.astype(F32)
    lane = lax.broadcasted_iota(jnp.int32, (q, 2 * SSD_HEADDIM), 1)
    lane1 = lane[0:1, :]
    last = 0 if rev else q - 1

    def pair(cols, p, lanes):
        rows = cols.shape[0]
        lo = jnp.broadcast_to(cols[:, 2 * p:2 * p + 1], (rows, 2 * SSD_HEADDIM))
        hi_ = jnp.broadcast_to(cols[:, 2 * p + 1:2 * p + 2], (rows, 2 * SSD_HEADDIM))
        return jnp.where(lanes < SSD_HEADDIM, lo, hi_)

    for gg in range(SSD_STEP_GROUPS):
        g = pl.program_id(1) * SSD_STEP_GROUPS + gg
        dtc = pltpu.roll(dtc_ref[...], lax.rem((128 // gh - g) * gh, 128), 1)
        a_lane = -jnp.exp(plane_ref[gg, 0:1, :])
        a_sub = -jnp.exp(psub_ref[gg, :, 0:1])
        dt_c = _softplus(dtc + plane_ref[gg, 1:2, :])
        dt_r = _softplus(dtr_ref[pl.ds(pl.multiple_of(g * gh, gh), gh), :] + psub_ref[gg, :, 1:2])
        acs_c = jnp.dot(tri_f, dt_c * a_lane, precision=hi, preferred_element_type=F32)
        acs_r = lax.dot_general(dt_r * a_sub, tri_f, NT_DIMS, precision=hi, preferred_element_type=F32)
        acs_last = acs_c[last:last + 1, :]
        to_end = jnp.exp(acs_last - acs_c) * dt_c
        eac = jnp.exp(acs_c)
        chunk_decay = jnp.exp(acs_last)
        bm = b_ref[:, gg * n_state:(gg + 1) * n_state].astype(BF16)
        cm = c_ref[:, gg * n_state:(gg + 1) * n_state].astype(BF16)
        cb = _dot_nt(cm, bm)
        y_off = _dot(cm, ht[gg].astype(BF16))
        decay_parts = []
        for p in range(gh // 2):
            lo_col = gg * gw + p * 2 * SSD_HEADDIM
            cols = slice(lo_col, lo_col + 2 * SSD_HEADDIM)
            pcols = slice(p * 2 * SSD_HEADDIM, (p + 1) * 2 * SSD_HEADDIM)
            xp = x_ref[:, cols]
            yp = y_off[:, pcols] * pair(eac, p, lane)
            for half in range(2):
                h = 2 * p + half
                seg = jnp.where(tri, acs_c[:, h:h + 1] - acs_r[h:h + 1, :], MASKED)
                w = cb * jnp.exp(seg) * dt_r[h:h + 1, :]
                in_half = (lane >= SSD_HEADDIM) if half else (lane < SSD_HEADDIM)
                yp = yp + _dot(w.astype(BF16), jnp.where(in_half, xp, 0.0).astype(BF16))
            y_ref[:, cols] = yp
            xw[:, cols] = (xp * pair(to_end, p, lane)).astype(BF16)
            decay_parts.append(pair(chunk_decay, p, lane1))
        st = lax.dot_general(bm, xw[:, gg * gw:(gg + 1) * gw], TN_DIMS, preferred_element_type=F32)
        ht[gg] = ht[gg] * jnp.concatenate(decay_parts, axis=1) + st

    @pl.when(c == nchunks - 1)
    def _():
        for gg in range(SSD_STEP_GROUPS):
            st_ref[gg] = ht[gg].T


def ssd_scan(xbc, dt_col, dt_row, plane, psub, h0t, *, row0, nseq, seq, rev, d_ssd, n_groups, d_state):
    q = SSD_CHUNK
    nchunks = seq // q
    width = SSD_GROUP_HEADS * SSD_HEADDIM
    rb0 = row0 // q
    bcol = d_ssd // d_state
    ccol = bcol + n_groups

    def chunk(c):
        return (nchunks - 1 - c) if rev else c

    sg = SSD_STEP_GROUPS
    assert n_groups % sg == 0 and bcol % sg == 0 and ccol % sg == 0
    kern = functools.partial(_ssd_kernel, rev=rev, nchunks=nchunks, n_state=d_state)
    return pl.pallas_call(
        kern,
        grid=(nseq, n_groups // sg, nchunks),
        in_specs=[
            pl.BlockSpec((q, sg * width), lambda b, g, c: (b * nchunks + chunk(c), g)),
            pl.BlockSpec((q, sg * d_state), lambda b, g, c: (b * nchunks + chunk(c), bcol // sg + g)),
            pl.BlockSpec((q, sg * d_state), lambda b, g, c: (b * nchunks + chunk(c), ccol // sg + g)),
            pl.BlockSpec((q, 128), lambda b, g, c: (rb0 + b * nchunks + chunk(c), 0)),
            pl.BlockSpec((n_groups * SSD_GROUP_HEADS, q), lambda b, g, c: (0, rb0 + b * nchunks + chunk(c))),
            pl.BlockSpec((sg, 8, 128), lambda b, g, c: (g, 0, 0)),
            pl.BlockSpec((sg, 8, 128), lambda b, g, c: (g, 0, 0)),
            pl.BlockSpec((None, sg, d_state, width), lambda b, g, c: (b, g, 0, 0)),
        ],
        out_specs=[
            pl.BlockSpec((q, sg * width), lambda b, g, c: (b * nchunks + chunk(c), g)),
            pl.BlockSpec((None, sg, width, d_state), lambda b, g, c: (b, g, 0, 0)),
        ],
        out_shape=[jax.ShapeDtypeStruct((nseq * seq, d_ssd), F32),
                   jax.ShapeDtypeStruct((nseq, n_groups, width, d_state), F32)],
        scratch_shapes=[pltpu.VMEM((sg, d_state, width), F32), pltpu.VMEM((q, sg * width), BF16)],
        compiler_params=_params("parallel", "parallel", "arbitrary"),
        name="ssd_scan",
    )(xbc, xbc, xbc, dt_col, dt_row, plane, psub, h0t)


def _ssd_out_kernel(yf_ref, yb_ref, x_ref, d_ref, g_ref, *rest):
    z_refs, o_ref = rest[:-1], rest[-1]
    z = jnp.concatenate([r[...] for r in z_refs], axis=1)
    y = yf_ref[...] + yb_ref[...] + d_ref[...] * x_ref[...]
    o_ref[...] = _rms(y * _silu(z), g_ref[...]).astype(o_ref.dtype)


def ssd_out(yf, yb, xbc, proj, d_vec, norm_g, *, row0, z_col):
    m, d_ssd = yf.shape
    tm = EW_ROWS
    rb0 = row0 // tm
    zw = math.gcd(z_col, d_ssd)
    row = pl.BlockSpec((tm, d_ssd), lambda i: (i, 0))
    vec = pl.BlockSpec((1, d_ssd), lambda i: (0, 0))
    z_specs = [pl.BlockSpec((tm, zw), lambda i, p=p: (rb0 + i, z_col // zw + p)) for p in range(d_ssd // zw)]
    return pl.pallas_call(
        _ssd_out_kernel,
        grid=(m // tm,),
        in_specs=[row, row, row, vec, vec] + z_specs,
        out_specs=row,
        out_shape=jax.ShapeDtypeStruct((m, d_ssd), BF16),
        compiler_params=_params("parallel"),
        name="ssd_out",
    )(yf, yb, xbc, d_vec.reshape(1, d_ssd), norm_g.reshape(1, d_ssd), *([proj] * (d_ssd // zw)))


def ssd_params(a_log, dt_bias, n_groups):
    gh = SSD_GROUP_HEADS
    al = a_log.reshape(2, n_groups, gh)
    db = dt_bias.reshape(2, n_groups, gh)
    plane = jnp.zeros((2, n_groups, 8, 128), F32).at[:, :, 0, :gh].set(al).at[:, :, 1, :gh].set(db)
    psub = jnp.zeros((2, n_groups, 8, 128), F32).at[:, :, :, 0].set(al).at[:, :, :, 1].set(db)
    return plane, psub


def ssd_mixer(proj, dt_col, dt_row, h0, conv_w, conv_b, a_log, dt_bias, d_skip, norm_g, *,
              row0, nseq, seq, z_col, xbc_col, d_ssd, n_groups, d_state):
    width = SSD_GROUP_HEADS * SSD_HEADDIM
    xbc = conv_silu(proj, conv_w, conv_b, row0=row0, nseq=nseq, seq=seq, col0=xbc_col)
    plane, psub = ssd_params(a_log, dt_bias, n_groups)
    if h0 is None:
        h0t = jnp.zeros((2, nseq, n_groups, d_state, width), F32)
    else:
        h0t = h0.astype(F32).reshape(nseq, 2, n_groups, width, d_state).transpose(1, 0, 2, 4, 3)
    ys, sts = [], []
    for d in range(2):
        y, st = ssd_scan(xbc, dt_col, dt_row, plane[d], psub[d], h0t[d], row0=row0, nseq=nseq, seq=seq,
                         rev=bool(d), d_ssd=d_ssd, n_groups=n_groups, d_state=d_state)
        ys.append(y)
        sts.append(st.reshape(nseq, n_groups * SSD_GROUP_HEADS, SSD_HEADDIM, d_state))
    d_vec = jnp.repeat(d_skip.astype(F32), SSD_HEADDIM)
    y = ssd_out(ys[0], ys[1], xbc, proj, d_vec, norm_g, row0=row0, z_col=z_col)
    return y, jnp.stack(sts, axis=1)


def kernel(x_prompt, x_sample, cache_na_k, cache_na_v, state_rglru, cache_swa_k, cache_swa_v, state_ssd, c, c_ctx, w_mod, b_mod, norm_mix_pre, norm_mix_post, norm_ffn_pre, norm_ffn_post, w_in_even, rg_conv_w, rg_conv_b, rg_w_a, rg_b_a, rg_w_i, rg_b_i, rg_lambda, na_rel_bias, w_in_odd, swa_sink, ssd_conv_w, ssd_conv_b, ssd_a_log, ssd_dt_bias, ssd_d, ssd_norm, w_out, ffn_w_up, ffn_conv_w, ffn_conv_b, ffn_w_down):
    nb_ctx, ctx_len, d = x_prompt.shape
    nb_lat, lat_len, _ = x_sample.shape
    depth = w_mod.shape[0]
    n_ctx = nb_ctx * ctx_len
    n_lat = nb_lat * lat_len
    d_a = rg_conv_w.shape[2]
    n_heads_b = na_rel_bias.shape[1]
    d_b = n_heads_b * HEAD_DIM
    n_heads_c = swa_sink.shape[1]
    n_kv_c = cache_swa_k.shape[2]
    grp_c = n_heads_c // n_kv_c
    d_c = n_heads_c * HEAD_DIM
    d_kv = n_kv_c * HEAD_DIM
    n_ssd_heads = ssd_d.shape[1]
    d_ssd = n_ssd_heads * SSD_HEADDIM
    d_state = state_ssd.shape[-1]
    n_groups = n_ssd_heads // SSD_GROUP_HEADS
    d_xbc = ssd_conv_w.shape[2]
    d_ff = ffn_w_down.shape[1]
    d_ff_pad = -(-d_ff // 1024) * 1024
    shapes = dict(n_ctx=n_ctx, lat_len=lat_len)

    x = jnp.concatenate([x_prompt.reshape(n_ctx, d), x_sample.reshape(n_lat, d)], axis=0)
    c8 = jnp.zeros((MOD_ROWS, d), F32).at[0].set(c_ctx).at[1:1 + nb_lat].set(c)
    mod = modulation_all(c8, w_mod, b_mod)
    cos2, sin2 = rope_tables(lat_len)

    def pad_cols(a, n):
        return jnp.pad(a, ((0, 0), (0, n - a.shape[1])))

    new_na_k, new_na_v, new_rglru, new_swa_k, new_swa_v, new_ssd = [], [], [], [], [], []
    h = adaln(x, mod[0], norm_mix_pre[0], shift_i=0, scale_i=1, **shapes)
    for layer in range(depth):
        if layer % 2 == 0:
            e = layer // 2
            proj = matmul(h, w_in_even[e].astype(BF16))
            rg = (rg_conv_w[e], rg_conv_b[e], rg_w_a[e].astype(BF16), rg_b_a[e],
                  rg_w_i[e].astype(BF16), rg_b_i[e], rg_lambda[e])
            ya_c, st = rglru_mixer(proj, jnp.zeros((nb_ctx, 2, d_a), F32), *rg,
                                   row0=0, nseq=nb_ctx, seq=ctx_len, d_a=d_a, heads=4)
            ya_l, _ = rglru_mixer(proj, state_rglru[:, e].astype(F32), *rg,
                                  row0=n_ctx, nseq=nb_lat, seq=lat_len, d_a=d_a, heads=1)
            cols = dict(q_col=2 * d_a, k_col=2 * d_a + d_b, v_col=2 * d_a + 2 * d_b)
            yb_c, k_ctx, v_ctx = ctx_attention(proj, jnp.zeros((n_heads_b,), F32), nseq=nb_ctx, seq=ctx_len,
                                               n_kv=n_heads_b, grp=1, use_sink=False, **cols)
            yb_l = na_attention(proj, cache_na_k[:, e], cache_na_v[:, e], na_bias_table(na_rel_bias[e]),
                                row0=n_ctx, nseq=nb_lat, seq=lat_len, n_heads=n_heads_b, **cols)
            new_na_k.append(k_ctx)
            new_na_v.append(v_ctx)
            new_rglru.append(st)
            o = jnp.concatenate([jnp.concatenate([ya_c, yb_c], axis=1),
                                 jnp.concatenate([ya_l, yb_l], axis=1)], axis=0)
        else:
            od = layer // 2
            w_in = w_in_odd[od]
            n_main = d_c + 2 * d_kv + d_ssd + d_xbc
            proj = matmul(h, w_in[:, :n_main].astype(BF16))
            w_dt = w_in[:, n_main:].astype(BF16)
            dt_col, dt_row = dt_proj(h, pad_cols(w_dt, 128), w_dt.T)
            cols = dict(q_col=0, k_col=d_c, v_col=d_c + d_kv)
            yc_c, k_ctx, v_ctx = ctx_attention(proj, swa_sink[od], nseq=nb_ctx, seq=ctx_len,
                                               n_kv=n_kv_c, grp=grp_c, use_sink=True, **cols)
            yc_l = swa_attention(proj, cache_swa_k[:, od], cache_swa_v[:, od], swa_sink[od], cos2, sin2,
                                 row0=n_ctx, nseq=nb_lat, seq=lat_len, n_kv=n_kv_c, grp=grp_c, **cols)
            ssd = (ssd_conv_w[od], ssd_conv_b[od], ssd_a_log[od], ssd_dt_bias[od], ssd_d[od], ssd_norm[od])
            scols = dict(z_col=d_c + 2 * d_kv, xbc_col=d_c + 2 * d_kv + d_ssd, d_ssd=d_ssd,
                         n_groups=n_groups, d_state=d_state)
            yd_c, st = ssd_mixer(proj, dt_col, dt_row, None, *ssd, row0=0, nseq=nb_ctx, seq=ctx_len, **scols)
            yd_l, _ = ssd_mixer(proj, dt_col, dt_row, state_ssd[:, od], *ssd,
                                row0=n_ctx, nseq=nb_lat, seq=lat_len, **scols)
            new_swa_k.append(k_ctx)
            new_swa_v.append(v_ctx)
            new_ssd.append(st)
            o = jnp.concatenate([jnp.concatenate([yc_c, yd_c], axis=1),
                                 jnp.concatenate([yc_l, yd_l], axis=1)], axis=0)
        y = matmul(o, w_out[layer].astype(BF16))
        x, h = resid_adaln(x, y, mod[layer], norm_mix_post[layer], norm_ffn_pre[layer], mod[layer],
                           gate_i=2, shift_i=3, scale_i=4, **shapes)
        w_up = ffn_w_up[layer]
        cw = jnp.concatenate([ffn_conv_w[layer], ffn_conv_b[layer][None]], axis=0)
        cg = jnp.pad(cw[:, :d_ff], ((0, 4), (0, d_ff_pad - d_ff)))
        cv = jnp.pad(cw[:, d_ff:], ((0, 4), (0, d_ff_pad - d_ff)))
        t = ffn_up(h, pad_cols(w_up[:, :d_ff].astype(BF16), d_ff_pad),
                   pad_cols(w_up[:, d_ff:].astype(BF16), d_ff_pad), cg, cv,
                   n_ctx=n_ctx, ctx_len=ctx_len, lat_len=lat_len)
        w_down = jnp.pad(ffn_w_down[layer].astype(BF16), ((0, d_ff_pad - d_ff), (0, 0)))
        f = matmul_ktiled(t, w_down, tk=d_ff_pad // 4)
        nxt = min(layer + 1, depth - 1)
        x, h = resid_adaln(x, f, mod[layer], norm_ffn_post[layer], norm_mix_pre[nxt], mod[nxt],
                           gate_i=5, shift_i=0, scale_i=1, **shapes)
    return (x[:n_ctx].reshape(nb_ctx, ctx_len, d), x[n_ctx:].reshape(nb_lat, lat_len, d),
            jnp.stack(new_na_k, axis=1), jnp.stack(new_na_v, axis=1), jnp.stack(new_rglru, axis=1),
            jnp.stack(new_swa_k, axis=1), jnp.stack(new_swa_v, axis=1), jnp.stack(new_ssd, axis=1))
```

```python
import functools
import math

import jax
import jax.numpy as jnp
from jax import lax
from jax.experimental import pallas as pl
from jax.experimental.pallas import tpu as pltpu

F32 = jnp.float32
BF16 = jnp.bfloat16

RMS_EPS = 1e-6
HEAD_DIM = 128
GRID_W = 64
NA_ROWS = 8
NA_COLS = 16
NA_QROWS = 4
SWA_BLOCK = 128
SWA_WINDOW = 128
ROPE_BASE = 10000.0
RG_C = 8.0
SSD_CHUNK = 128
SSD_HEADDIM = 64
SSD_GROUP_HEADS = 8
MASKED = -1e30
MOD_ROWS = 8
V7X_VMEM_BYTES = 64 * 1024 * 1024
VMEM_LIMIT = V7X_VMEM_BYTES - 8 * 1024 * 1024
ROW_TILE = 1024
COL_TILE = 1024
EW_ROWS = 256
NT_DIMS = (((1,), (1,)), ((), ()))
TN_DIMS = (((0,), (0,)), ((), ()))


def _params(*sem):
    return pltpu.CompilerParams(dimension_semantics=sem, vmem_limit_bytes=VMEM_LIMIT)


def _sigmoid(x):
    return 0.5 * jnp.tanh(0.5 * x) + 0.5


def _silu(x):
    return x * _sigmoid(x)


def _softplus(x):
    return jnp.maximum(x, 0.0) + jnp.log(1.0 + jnp.exp(-jnp.abs(x)))


def _gelu_tanh(x):
    return 0.5 * x * (1.0 + jnp.tanh(math.sqrt(2.0 / math.pi) * (x + 0.044715 * (x * x * x))))


def _rms(x, g):
    ms = jnp.mean(x * x, axis=-1, keepdims=True)
    return x * lax.rsqrt(ms + RMS_EPS) * g


def _dot(a, b):
    return jnp.dot(a, b, preferred_element_type=F32)


def _dot_nt(a, b):
    return lax.dot_general(a, b, NT_DIMS, preferred_element_type=F32)


def _mod_kernel(c_ref, w_ref, b_ref, o_ref):
    s = _silu(c_ref[...])
    o_ref[...] = _dot(s.astype(BF16), w_ref[...].astype(BF16)) + b_ref[...]


def modulation_all(c8, w_mod, b_mod):
    depth, d, n6 = w_mod.shape
    tn = 512
    per = d // tn
    return pl.pallas_call(
        _mod_kernel,
        grid=(depth, n6 // tn),
        in_specs=[
            pl.BlockSpec((MOD_ROWS, d), lambda l, j: (0, 0)),
            pl.BlockSpec((None, d, tn), lambda l, j: (l, 0, j)),
            pl.BlockSpec((None, 1, tn), lambda l, j: (l, 0, j)),
        ],
        out_specs=pl.BlockSpec((None, None, MOD_ROWS, tn), lambda l, j: (l, j // per, 0, j % per)),
        out_shape=jax.ShapeDtypeStruct((depth, 6, MOD_ROWS, d), F32),
        compiler_params=_params("parallel", "parallel"),
        name="modulation",
    )(c8, w_mod, b_mod.reshape(depth, 1, n6))


def _mod_row(i, tm, n_ctx, lat_len):
    row0 = i * tm
    return jnp.where(row0 < n_ctx, 0, 1 + (row0 - n_ctx) // lat_len)


def _group_specs(pair, tm, d, nc):
    lat0 = nc if pair[0] is pair[1] else 0
    return [pl.BlockSpec((tm, d), lambda i: (jnp.minimum(i, nc - 1), 0)),
            pl.BlockSpec((tm, d), lambda i: (lat0 + jnp.maximum(i - nc, 0), 0))]


def _adaln_kernel(xc_ref, xl_ref, mod_ref, g_ref, h_ref, *, tm, n_ctx, lat_len, shift_i, scale_i):
    i = pl.program_id(0)
    r = _mod_row(i, tm, n_ctx, lat_len)
    x = jnp.where(i * tm < n_ctx, xc_ref[...], xl_ref[...])
    shift = mod_ref[shift_i, pl.ds(r, 1), :]
    scale = mod_ref[scale_i, pl.ds(r, 1), :]
    h_ref[...] = (_rms(x, g_ref[...]) * (1.0 + scale) + shift).astype(h_ref.dtype)


def adaln(x_pair, mod, g, *, n_ctx, lat_len, shift_i, scale_i):
    d = x_pair[0].shape[1]
    tm = EW_ROWS
    nc = n_ctx // tm
    m = n_ctx + x_pair[1].shape[0] - (n_ctx if x_pair[0] is x_pair[1] else 0)
    kern = functools.partial(_adaln_kernel, tm=tm, n_ctx=n_ctx, lat_len=lat_len,
                             shift_i=shift_i, scale_i=scale_i)
    return pl.pallas_call(
        kern,
        grid=(m // tm,),
        in_specs=_group_specs(x_pair, tm, d, nc) + [
            pl.BlockSpec((None, 6, MOD_ROWS, d), lambda i: (mod[1], 0, 0, 0)),
            pl.BlockSpec((1, d), lambda i: (0, 0)),
        ],
        out_specs=pl.BlockSpec((tm, d), lambda i: (i, 0)),
        out_shape=jax.ShapeDtypeStruct((m, d), BF16),
        compiler_params=_params("parallel"),
        name="adaln",
    )(*x_pair, mod[0], g.reshape(1, d))


def _resid_kernel(xc_ref, xl_ref, yc_ref, yl_ref, mod_ref, gpost_ref, gpre_ref, modn_ref,
                  xoc_ref, xol_ref, h_ref, *, tm, n_ctx, lat_len, gate_i, shift_i, scale_i):
    i = pl.program_id(0)
    is_ctx = i * tm < n_ctx
    r = _mod_row(i, tm, n_ctx, lat_len)
    gate = mod_ref[gate_i, pl.ds(r, 1), :]
    y = jnp.where(is_ctx, yc_ref[...], yl_ref[...])
    x = jnp.where(is_ctx, xc_ref[...], xl_ref[...]) + gate * _rms(y, gpost_ref[...])

    @pl.when(is_ctx)
    def _():
        xoc_ref[...] = x

    @pl.when(jnp.logical_not(is_ctx))
    def _():
        xol_ref[...] = x

    shift = modn_ref[shift_i, pl.ds(r, 1), :]
    scale = modn_ref[scale_i, pl.ds(r, 1), :]
    h_ref[...] = (_rms(x, gpre_ref[...]) * (1.0 + scale) + shift).astype(h_ref.dtype)


def resid_adaln(x_pair, y_pair, mod, gpost, gpre, modn, *, n_ctx, lat_len, gate_i, shift_i, scale_i):
    d = x_pair[0].shape[1]
    tm = EW_ROWS // 2
    nc = n_ctx // tm
    n_lat = x_pair[1].shape[0] - (n_ctx if x_pair[0] is x_pair[1] else 0)
    m = n_ctx + n_lat
    kern = functools.partial(_resid_kernel, tm=tm, n_ctx=n_ctx, lat_len=lat_len,
                             gate_i=gate_i, shift_i=shift_i, scale_i=scale_i)
    vec = pl.BlockSpec((1, d), lambda i: (0, 0))

    def modspec(layer):
        return pl.BlockSpec((None, 6, MOD_ROWS, d), lambda i: (layer, 0, 0, 0))

    return pl.pallas_call(
        kern,
        grid=(m // tm,),
        in_specs=(_group_specs(x_pair, tm, d, nc) + _group_specs(y_pair, tm, d, nc)
                  + [modspec(mod[1]), vec, vec, modspec(modn[1])]),
        out_specs=[pl.BlockSpec((tm, d), lambda i: (jnp.minimum(i, nc - 1), 0)),
                   pl.BlockSpec((tm, d), lambda i: (jnp.maximum(i - nc, 0), 0)),
                   pl.BlockSpec((tm, d), lambda i: (i, 0))],
        out_shape=[jax.ShapeDtypeStruct((n_ctx, d), F32), jax.ShapeDtypeStruct((n_lat, d), F32),
                   jax.ShapeDtypeStruct((m, d), BF16)],
        compiler_params=_params("arbitrary"),
        name="resid_adaln",
    )(*x_pair, *y_pair, mod[0], gpost.reshape(1, d), gpre.reshape(1, d), modn[0])


def _mm_kernel(a_ref, w_ref, o_ref):
    o_ref[...] = _dot(a_ref[...], w_ref[...]).astype(o_ref.dtype)


def matmul(a, w, layer, n=None, out_dtype=F32):
    m, k = a.shape
    n = w.shape[2] if n is None else n
    tm, tn = min(ROW_TILE, m), min(COL_TILE, n)
    return pl.pallas_call(
        _mm_kernel,
        grid=(m // tm, n // tn),
        in_specs=[pl.BlockSpec((tm, k), lambda i, j: (i, 0)),
                  pl.BlockSpec((None, k, tn), lambda i, j: (layer, 0, j))],
        out_specs=pl.BlockSpec((tm, tn), lambda i, j: (i, j)),
        out_shape=jax.ShapeDtypeStruct((m, n), out_dtype),
        compiler_params=_params("parallel", "parallel"),
        name="matmul",
    )(a, w)


def _mm_pair_kernel(a1_ref, a2_ref, w1_ref, w2_ref, o_ref):
    o_ref[...] = _dot(a1_ref[...], w1_ref[...]) + _dot(a2_ref[...], w2_ref[...])


def matmul_pair(a1, a2, w, layer):
    m, k = a1.shape
    n = w.shape[2]
    assert a2.shape == a1.shape and w.shape[1] == 2 * k
    tm, tn = min(ROW_TILE, m), min(COL_TILE, n)
    return pl.pallas_call(
        _mm_pair_kernel,
        grid=(m // tm, n // tn),
        in_specs=[pl.BlockSpec((tm, k), lambda i, j: (i, 0)),
                  pl.BlockSpec((tm, k), lambda i, j: (i, 0)),
                  pl.BlockSpec((None, k, tn), lambda i, j: (layer, 0, j)),
                  pl.BlockSpec((None, k, tn), lambda i, j: (layer, 1, j))],
        out_specs=pl.BlockSpec((tm, tn), lambda i, j: (i, j)),
        out_shape=jax.ShapeDtypeStruct((m, n), F32),
        compiler_params=_params("parallel", "parallel"),
        name="matmul_pair",
    )(a1, a2, w, w)


def _mm_acc_kernel(a_ref, w_ref, o_ref):
    @pl.when(pl.program_id(2) == 0)
    def _():
        o_ref[...] = jnp.zeros_like(o_ref)

    o_ref[...] += _dot(a_ref[...], w_ref[...])


def matmul_ktiled(a, w, layer, tk):
    m, k = a.shape
    n = w.shape[2]
    tm, tn = min(ROW_TILE, m), min(COL_TILE, n)
    return pl.pallas_call(
        _mm_acc_kernel,
        grid=(m // tm, n // tn, k // tk),
        in_specs=[pl.BlockSpec((tm, tk), lambda i, j, kk: (i, kk)),
                  pl.BlockSpec((None, tk, tn), lambda i, j, kk: (layer, kk, j))],
        out_specs=pl.BlockSpec((tm, tn), lambda i, j, kk: (i, j)),
        out_shape=jax.ShapeDtypeStruct((m, n), F32),
        compiler_params=_params("parallel", "parallel", "arbitrary"),
        name="matmul_ktiled",
    )(a, w)


CAST_TILE = 256


def _cast_pad_kernel(w_ref, o_ref, *, n_real, axis):
    @pl.when(pl.program_id(axis) < n_real)
    def _():
        o_ref[...] = w_ref[...].astype(o_ref.dtype)

    @pl.when(pl.program_id(axis) >= n_real)
    def _():
        o_ref[...] = jnp.zeros_like(o_ref)


def cast_split_pad_cols(w, n_pad):
    depth, k, n2 = w.shape
    n = n2 // 2
    t = CAST_TILE
    assert n % t == 0 and n_pad % t == 0
    n_real = n // t
    kern = functools.partial(_cast_pad_kernel, n_real=n_real, axis=2)
    return pl.pallas_call(
        kern,
        grid=(depth, 2, n_pad // t),
        in_specs=[pl.BlockSpec((None, k, t), lambda l, s, j: (l, 0, s * n_real + jnp.minimum(j, n_real - 1)))],
        out_specs=pl.BlockSpec((None, None, k, t), lambda l, s, j: (l, s, 0, j)),
        out_shape=jax.ShapeDtypeStruct((depth, 2, k, n_pad), BF16),
        compiler_params=_params("parallel", "parallel", "parallel"),
        name="cast_split_pad_cols",
    )(w)


def cast_pad_rows(w, k_pad):
    depth, k, n = w.shape
    t = CAST_TILE
    assert k % t == 0 and k_pad % t == 0
    n_real = k // t
    kern = functools.partial(_cast_pad_kernel, n_real=n_real, axis=1)
    return pl.pallas_call(
        kern,
        grid=(depth, k_pad // t),
        in_specs=[pl.BlockSpec((None, t, n), lambda l, j: (l, jnp.minimum(j, n_real - 1), 0))],
        out_specs=pl.BlockSpec((None, t, n), lambda l, j: (l, j, 0)),
        out_shape=jax.ShapeDtypeStruct((depth, k_pad, n), BF16),
        compiler_params=_params("parallel", "parallel"),
        name="cast_pad_rows",
    )(w)


def _dt_kernel(h_ref, w_ref, wt_ref, oc_ref, or_ref):
    h = h_ref[...]
    oc_ref[...] = _dot(h, w_ref[...])
    or_ref[...] = _dot_nt(wt_ref[...], h)


def dt_proj(h, w_pad, w_t):
    m, k = h.shape
    nh = w_t.shape[0]
    tm = min(ROW_TILE, m)
    return pl.pallas_call(
        _dt_kernel,
        grid=(m // tm,),
        in_specs=[pl.BlockSpec((tm, k), lambda i: (i, 0)),
                  pl.BlockSpec((k, 128), lambda i: (0, 0)),
                  pl.BlockSpec((nh, k), lambda i: (0, 0))],
        out_specs=[pl.BlockSpec((tm, 128), lambda i: (i, 0)),
                   pl.BlockSpec((nh, tm), lambda i: (0, i))],
        out_shape=[jax.ShapeDtypeStruct((m, 128), F32), jax.ShapeDtypeStruct((nh, m), F32)],
        compiler_params=_params("parallel"),
        name="dt_proj",
    )(h, w_pad, w_t)


FFN_HALO = 16
FFN_PARTS = 1


def _ffn_up_kernel(a_ref, ap_ref, an_ref, wg_ref, wv_ref, cg_ref, cv_ref, o_ref, aext, *,
                   tm, n_ctx, ctx_len, lat_len):
    i = pl.program_id(0)

    @pl.when(pl.program_id(1) == 0)
    def _():
        aext[0:FFN_HALO, :] = ap_ref[...]
        aext[FFN_HALO:FFN_HALO + tm, :] = a_ref[...]
        aext[FFN_HALO + tm:, :] = an_ref[...]

    a = aext[...]
    local = lax.broadcasted_iota(jnp.int32, (tm, 1), 0)
    is_ctx = i * tm < n_ctx
    pos = jnp.where(is_ctx, lax.rem(local, ctx_len), lax.rem(i * tm - n_ctx, lat_len) + local)
    has_prev = pos != 0
    has_next = pos != jnp.where(is_ctx, ctx_len - 1, lat_len - 1)
    rows = tm + 2 * FFN_HALO

    def conv(u, c_ref, cols):
        up = pltpu.roll(u, 1, 0)[FFN_HALO:FFN_HALO + tm]
        un = pltpu.roll(u, rows - 1, 0)[FFN_HALO:FFN_HALO + tm]
        uc = u[FFN_HALO:FFN_HALO + tm]
        return (c_ref[0:1, cols] * jnp.where(has_prev, up, 0.0) + c_ref[1:2, cols] * uc
                + c_ref[2:3, cols] * jnp.where(has_next, un, 0.0) + c_ref[3:4, cols])

    part = o_ref.shape[1] // FFN_PARTS
    for p in range(FFN_PARTS):
        cols = slice(p * part, (p + 1) * part)
        g = conv(_dot(a, wg_ref[:, cols]), cg_ref, cols)
        v = conv(_dot(a, wv_ref[:, cols]), cv_ref, cols)
        o_ref[:, cols] = (_silu(g) * v).astype(o_ref.dtype)


def ffn_up(h, w, conv, layer, *, n_ctx, ctx_len, lat_len):
    m, k = h.shape
    n = w.shape[3]
    tm = min(ROW_TILE, m)
    tn = 512
    assert tm % ctx_len == 0 and lat_len % tm == 0 and n_ctx % tm == 0
    nhalo = m // FFN_HALO
    per = tm // FFN_HALO
    kern = functools.partial(_ffn_up_kernel, tm=tm, n_ctx=n_ctx, ctx_len=ctx_len, lat_len=lat_len)
    return pl.pallas_call(
        kern,
        grid=(m // tm, n // tn),
        in_specs=[
            pl.BlockSpec((tm, k), lambda i, j: (i, 0)),
            pl.BlockSpec((FFN_HALO, k), lambda i, j: (jnp.maximum(i * per - 1, 0), 0)),
            pl.BlockSpec((FFN_HALO, k), lambda i, j: (jnp.minimum((i + 1) * per, nhalo - 1), 0)),
            pl.BlockSpec((None, None, k, tn), lambda i, j: (layer, 0, 0, j)),
            pl.BlockSpec((None, None, k, tn), lambda i, j: (layer, 1, 0, j)),
            pl.BlockSpec((None, None, 8, tn), lambda i, j: (layer, 0, 0, j)),
            pl.BlockSpec((None, None, 8, tn), lambda i, j: (layer, 1, 0, j)),
        ],
        out_specs=pl.BlockSpec((tm, tn), lambda i, j: (i, j)),
        out_shape=jax.ShapeDtypeStruct((m, n), BF16),
        scratch_shapes=[pltpu.VMEM((tm + 2 * FFN_HALO, k), BF16)],
        compiler_params=_params("parallel", "arbitrary"),
        name="ffn_up",
    )(h, h, h, w, w, conv, conv)


RG_CHUNK = 256
RG_PAD = 8


def _scan_chunk(a, u, h, reverse):
    nblk = a.shape[0] // 8
    row8 = lax.broadcasted_iota(jnp.int32, (8, a.shape[1]), 0)
    outs = [None] * nblk
    for b in (range(nblk - 1, -1, -1) if reverse else range(nblk)):
        aa = a[8 * b:8 * b + 8]
        uu = u[8 * b:8 * b + 8]
        for s in (1, 2, 4):
            keep = (row8 < 8 - s) if reverse else (row8 >= s)
            shift = 8 - s if reverse else s
            a_sh = jnp.where(keep, pltpu.roll(aa, shift, 0), 1.0)
            u_sh = jnp.where(keep, pltpu.roll(uu, shift, 0), 0.0)
            uu = aa * u_sh + uu
            aa = aa * a_sh
        hb = aa * h + uu
        outs[b] = hb
        h = hb[0:1] if reverse else hb[7:8]
    return jnp.concatenate(outs, axis=0), h


def _rglru_kernel(ga_ref, xa_ref, h0_ref, cw_ref, cb_ref, wa_ref, ba_ref, wi_ref, bi_ref, lam_ref,
                  y_ref, st_ref, xpad, hf, *, seq, heads):
    chunk = min(RG_CHUNK, seq)
    nchunks = seq // chunk
    width = heads * HEAD_DIM
    xpad[0:RG_PAD, :] = jnp.zeros((RG_PAD, width), F32)
    xpad[RG_PAD + seq:, :] = jnp.zeros((RG_PAD, width), F32)
    xpad[RG_PAD:RG_PAD + seq, :] = xa_ref[...]
    win_rows = chunk + 2 * RG_PAD

    for hh in range(heads):
        lanes = slice(hh * HEAD_DIM, (hh + 1) * HEAD_DIM)
        rate = [(-RG_C) * _softplus(-lam_ref[d:d + 1, lanes]) for d in range(2)]

        def coeffs(t0, d, lanes=lanes, hh=hh, rate=rate):
            win = xpad[pl.ds(t0, win_rows), lanes]

            def at(off):
                return pltpu.roll(win, (win_rows - off) % win_rows, 0)[RG_PAD:RG_PAD + chunk]

            xc = (cw_ref[0:1, lanes] * at(-2) + cw_ref[1:2, lanes] * at(-1)
                  + cw_ref[2:3, lanes] * at(0) + cw_ref[3:4, lanes] * at(1) + cb_ref[0:1, lanes])
            xb = xc.astype(BF16)
            r = _sigmoid(_dot(xb, wa_ref[d, hh]) + ba_ref[d:d + 1, lanes])
            gate_i = _sigmoid(_dot(xb, wi_ref[d, hh]) + bi_ref[d:d + 1, lanes])
            a = jnp.exp(r * rate[d])
            u = jnp.sqrt(1.0 - a * a) * (gate_i * xc)
            return a, u

        def fwd(c, h, lanes=lanes, coeffs=coeffs):
            t0 = pl.multiple_of(c * chunk, chunk)
            a, u = coeffs(t0, 0)
            hs, h = _scan_chunk(a, u, h, reverse=False)
            hf[pl.ds(t0, chunk), lanes] = hs
            return h

        h_f = lax.fori_loop(0, nchunks, fwd, h0_ref[0:1, lanes])

        def bwd(cc, h, lanes=lanes, coeffs=coeffs):
            t0 = pl.multiple_of((nchunks - 1 - cc) * chunk, chunk)
            a, u = coeffs(t0, 1)
            hs, h = _scan_chunk(a, u, h, reverse=True)
            y = (hf[pl.ds(t0, chunk), lanes] + hs) * _gelu_tanh(ga_ref[pl.ds(t0, chunk), lanes])
            y_ref[pl.ds(t0, chunk), lanes] = y.astype(y_ref.dtype)
            return h

        h_b = lax.fori_loop(0, nchunks, bwd, h0_ref[1:2, lanes])
        st_ref[0:1, lanes] = h_f
        st_ref[1:2, lanes] = h_b


def rglru_mixer(proj, h0, conv_w, conv_b, w_a, b_a, w_i, b_i, lam, *, row0, nseq, seq, d_a, heads):
    width = heads * HEAD_DIM
    ncol = d_a // width
    rb0 = row0 // seq
    kern = functools.partial(_rglru_kernel, seq=seq, heads=heads)
    vec2 = pl.BlockSpec((2, width), lambda b, j: (0, j))
    return pl.pallas_call(
        kern,
        grid=(nseq, ncol),
        in_specs=[
            pl.BlockSpec((seq, width), lambda b, j: (rb0 + b, j)),
            pl.BlockSpec((seq, width), lambda b, j: (rb0 + b, ncol + j)),
            pl.BlockSpec((None, 2, width), lambda b, j: (b, 0, j)),
            pl.BlockSpec((4, width), lambda b, j: (0, j)),
            pl.BlockSpec((1, width), lambda b, j: (0, j)),
            pl.BlockSpec((2, heads, HEAD_DIM, HEAD_DIM), lambda b, j: (0, j, 0, 0)),
            vec2,
            pl.BlockSpec((2, heads, HEAD_DIM, HEAD_DIM), lambda b, j: (0, j, 0, 0)),
            vec2,
            vec2,
        ],
        out_specs=[pl.BlockSpec((seq, width), lambda b, j: (b, j)),
                   pl.BlockSpec((None, 2, width), lambda b, j: (b, 0, j))],
        out_shape=[jax.ShapeDtypeStruct((nseq * seq, d_a), BF16),
                   jax.ShapeDtypeStruct((nseq, 2, d_a), F32)],
        scratch_shapes=[pltpu.VMEM((seq + 2 * RG_PAD, width), F32), pltpu.VMEM((seq, width), F32)],
        compiler_params=_params("parallel", "parallel"),
        name="rglru",
    )(proj, proj, h0, conv_w, conv_b.reshape(1, d_a), w_a, b_a, w_i, b_i, lam)


def _sink_column(sink_ref, first_head, grp, rows_per_head):
    rows = lax.broadcasted_iota(jnp.int32, (grp * rows_per_head, 1), 0)
    col = jnp.zeros((grp * rows_per_head, 1), F32)
    for g in range(grp):
        in_g = (rows >= g * rows_per_head) & (rows < (g + 1) * rows_per_head)
        col = jnp.where(in_g, sink_ref[first_head + g], col)
    return col


def _ctx_attn_kernel(sink_ref, q_ref, k_ref, v_ref, *rest, nkv, grp, use_sink):
    o_ref, ko_ref, vo_ref = rest[-3:]
    seq = q_ref.shape[0]
    scale = HEAD_DIM ** -0.5
    for kk in range(nkv):
        k = k_ref[:, kk * HEAD_DIM:(kk + 1) * HEAD_DIM]
        v = v_ref[:, kk * HEAD_DIM:(kk + 1) * HEAD_DIM]
        ko_ref[kk] = k
        vo_ref[kk] = v
        qs = jnp.concatenate(
            [q_ref[:, (kk * grp + g) * HEAD_DIM:(kk * grp + g + 1) * HEAD_DIM] for g in range(grp)], axis=0)
        s = _dot_nt((qs * scale).astype(BF16), k.astype(BF16))
        m = jnp.max(s, axis=-1, keepdims=True)
        if use_sink:
            first = (pl.program_id(1) * nkv + kk) * grp
            sk = _sink_column(sink_ref, first, grp, seq)
            m = jnp.maximum(m, sk)
        p = jnp.exp(s - m)
        den = jnp.sum(p, axis=-1, keepdims=True)
        if use_sink:
            den = den + jnp.exp(sk - m)
        o = _dot(p.astype(BF16), v.astype(BF16)) / den
        for g in range(grp):
            hq = kk * grp + g
            o_ref[:, hq * HEAD_DIM:(hq + 1) * HEAD_DIM] = o[g * seq:(g + 1) * seq].astype(o_ref.dtype)


def ctx_attention(proj, sink, caches, slot, n_slots, *, nseq, seq, q_col, k_col, v_col, n_kv, grp, use_sink):
    nkv = max(1, 4 // grp)
    qw = nkv * grp * HEAD_DIM
    kw = nkv * HEAD_DIM
    kern = functools.partial(_ctx_attn_kernel, nkv=nkv, grp=grp, use_sink=use_sink)
    cache_spec = pl.BlockSpec((None, None, nkv, seq, HEAD_DIM), lambda b, j: (b, slot, j, 0, 0))
    cache_shape = jax.ShapeDtypeStruct((nseq, n_slots, n_kv, seq, HEAD_DIM), F32)
    prev = [] if caches is None else list(caches)
    return pl.pallas_call(
        kern,
        grid=(nseq, n_kv // nkv),
        in_specs=[
            pl.BlockSpec(memory_space=pltpu.SMEM),
            pl.BlockSpec((seq, qw), lambda b, j: (b, q_col // qw + j)),
            pl.BlockSpec((seq, kw), lambda b, j: (b, k_col // kw + j)),
            pl.BlockSpec((seq, kw), lambda b, j: (b, v_col // kw + j)),
        ] + [pl.BlockSpec(memory_space=pl.ANY)] * len(prev),
        out_specs=[pl.BlockSpec((seq, qw), lambda b, j: (b, j)), cache_spec, cache_spec],
        out_shape=[jax.ShapeDtypeStruct((nseq * seq, n_kv * grp * HEAD_DIM), BF16), cache_shape, cache_shape],
        input_output_aliases={4: 1, 5: 2} if prev else {},
        compiler_params=_params("parallel", "parallel"),
        name="ctx_attention",
    )(sink, proj, proj, proj, *prev)


def na_bias_table(rel_bias):
    nh = rel_bias.shape[0]
    nk = 3 * NA_QROWS
    padded = jnp.pad(rel_bias.astype(F32), ((0, 0), (0, 0), (GRID_W - NA_COLS, GRID_W - NA_COLS)))
    toep = jnp.stack([padded[:, :, GRID_W - 1 - c:2 * GRID_W - 1 - c] for c in range(GRID_W)], axis=2)
    c = jnp.arange(GRID_W)[:, None]
    j = jnp.arange(GRID_W)[None, :]
    cs = jnp.clip(c - NA_COLS // 2, 0, GRID_W - NA_COLS)
    toep = jnp.where((j >= cs) & (j < cs + NA_COLS), toep, MASKED)
    d0 = NA_ROWS - 1 - NA_QROWS
    per_q = jnp.stack([toep[:, d0 - qi:d0 - qi + nk].transpose(0, 2, 1, 3) for qi in range(NA_QROWS)],
                      axis=1)
    qi = jnp.arange(NA_QROWS)[:, None]
    ki = jnp.arange(nk)[None, :]
    row_ok = jnp.stack([
        (ki >= NA_QROWS) & (ki < NA_QROWS + NA_ROWS) & (qi >= 0),
        (ki >= qi) & (ki < qi + NA_ROWS),
        (ki >= 0) & (ki < NA_ROWS) & (qi >= 0),
    ])
    tab = jnp.where(row_ok[:, None, :, None, :, None], per_q[None], MASKED)
    return tab.reshape(3, nh, NA_QROWS * GRID_W, nk * GRID_W)


NA_STEP_HEADS = 2


def _na_kernel(q_ref, kp_ref, kc_ref, kn_ref, vp_ref, vc_ref, vn_ref, kx_ref, vx_ref, bias_ref, o_ref):
    for hh in range(NA_STEP_HEADS):
        cols = slice(hh * HEAD_DIM, (hh + 1) * HEAD_DIM)
        q = (q_ref[:, cols] * HEAD_DIM ** -0.5).astype(BF16)
        kl = jnp.concatenate([kp_ref[:, cols], kc_ref[:, cols], kn_ref[:, cols]], axis=0).astype(BF16)
        vl = jnp.concatenate([vp_ref[:, cols], vc_ref[:, cols], vn_ref[:, cols]], axis=0).astype(BF16)
        s_loc = _dot_nt(q, kl) + bias_ref[hh]
        s_ctx = _dot_nt(q, kx_ref[hh].astype(BF16))
        m = jnp.maximum(jnp.max(s_loc, axis=-1, keepdims=True), jnp.max(s_ctx, axis=-1, keepdims=True))
        p_loc = jnp.exp(s_loc - m)
        p_ctx = jnp.exp(s_ctx - m)
        den = jnp.sum(p_loc, axis=-1, keepdims=True) + jnp.sum(p_ctx, axis=-1, keepdims=True)
        o = _dot(p_loc.astype(BF16), vl) + _dot(p_ctx.astype(BF16), vx_ref[hh].astype(BF16))
        o_ref[:, cols] = (o / den).astype(o_ref.dtype)


def na_attention(proj, k_ctx, v_ctx, bias_tab, *, row0, nseq, seq, q_col, k_col, v_col, n_heads):
    blk = NA_QROWS * GRID_W
    nb = seq // blk
    rb0 = row0 // blk
    past = k_ctx.shape[2]
    hs = NA_STEP_HEADS
    width = hs * HEAD_DIM
    qc, kc, vc = q_col // width, k_col // width, v_col // width

    def rows(b, i):
        return rb0 + b * nb + i

    def spec(col, delta):
        return pl.BlockSpec((blk, width),
                            lambda h, b, i: (rows(b, jnp.clip(i + delta, 0, nb - 1)), col + h))

    def block_type(i):
        return jnp.where(i == 0, 0, jnp.where(i == nb - 1, 2, 1))

    cache = pl.BlockSpec((None, hs, past, HEAD_DIM), lambda h, b, i: (b, h, 0, 0))
    return pl.pallas_call(
        _na_kernel,
        grid=(n_heads // hs, nseq, nb),
        in_specs=[spec(qc, 0), spec(kc, -1), spec(kc, 0), spec(kc, 1),
                  spec(vc, -1), spec(vc, 0), spec(vc, 1), cache, cache,
                  pl.BlockSpec((None, hs, blk, 3 * blk), lambda h, b, i: (block_type(i), h, 0, 0))],
        out_specs=pl.BlockSpec((blk, width), lambda h, b, i: (b * nb + i, h)),
        out_shape=jax.ShapeDtypeStruct((nseq * seq, n_heads * HEAD_DIM), BF16),
        compiler_params=_params("parallel", "parallel", "parallel"),
        name="na_attention",
    )(proj, proj, proj, proj, proj, proj, proj, k_ctx, v_ctx, bias_tab)


def rope_tables(seq):
    t = jnp.arange(seq)
    row = (t // GRID_W).astype(F32)
    col = (t % GRID_W).astype(F32)
    n = HEAD_DIM // 4
    inv = ROPE_BASE ** (-jnp.arange(n, dtype=F32) / n)
    ang = jnp.concatenate([row[:, None] * inv, col[:, None] * inv], axis=-1)
    cos, sin = jnp.cos(ang), jnp.sin(ang)
    return jnp.concatenate([cos, cos], axis=-1), jnp.concatenate([-sin, sin], axis=-1)


SWA_STEP_KV = 2


def _rope(x, cos, sin):
    return x * cos + pltpu.roll(x, HEAD_DIM // 2, 1) * sin


def _swa_kernel(sink_ref, q_ref, kp_ref, kc_ref, kn_ref, vp_ref, vc_ref, vn_ref, kx_ref, vx_ref,
                cos_ref, sin_ref, o_ref, *, grp, nb):
    kvh = pl.program_id(0)
    i = pl.program_id(2)
    blk = SWA_BLOCK

    def tables(ib):
        t0 = pl.multiple_of(ib * blk, blk)
        return cos_ref[pl.ds(t0, blk), :], sin_ref[pl.ds(t0, blk), :]

    cq, sq = tables(i)
    cp, sp = tables(jnp.maximum(i - 1, 0))
    cn, sn = tables(jnp.minimum(i + 1, nb - 1))
    scale = HEAD_DIM ** -0.5
    qi = lax.broadcasted_iota(jnp.int32, (blk, 3 * blk), 0)
    kj = lax.broadcasted_iota(jnp.int32, (blk, 3 * blk), 1)
    ok = (jnp.abs(kj - blk - qi) <= SWA_WINDOW)
    ok = ok & ((kj >= blk) | (i > 0)) & ((kj < 2 * blk) | (i < nb - 1))
    okg = jnp.concatenate([ok] * grp, axis=0)
    for kk in range(SWA_STEP_KV):
        kcols = slice(kk * HEAD_DIM, (kk + 1) * HEAD_DIM)
        q0 = kk * grp
        qs = jnp.concatenate(
            [_rope(q_ref[:, (q0 + g) * HEAD_DIM:(q0 + g + 1) * HEAD_DIM], cq, sq) * scale for g in range(grp)],
            axis=0).astype(BF16)
        kl = jnp.concatenate([_rope(kp_ref[:, kcols], cp, sp), _rope(kc_ref[:, kcols], cq, sq),
                              _rope(kn_ref[:, kcols], cn, sn)], axis=0).astype(BF16)
        vl = jnp.concatenate([vp_ref[:, kcols], vc_ref[:, kcols], vn_ref[:, kcols]], axis=0).astype(BF16)
        s_loc = jnp.where(okg, _dot_nt(qs, kl), MASKED)
        s_ctx = _dot_nt(qs, kx_ref[kk].astype(BF16))
        sk = _sink_column(sink_ref, (kvh * SWA_STEP_KV + kk) * grp, grp, blk)
        m = jnp.maximum(jnp.maximum(jnp.max(s_loc, axis=-1, keepdims=True),
                                    jnp.max(s_ctx, axis=-1, keepdims=True)), sk)
        p_loc = jnp.exp(s_loc - m)
        p_ctx = jnp.exp(s_ctx - m)
        den = (jnp.sum(p_loc, axis=-1, keepdims=True) + jnp.sum(p_ctx, axis=-1, keepdims=True)
               + jnp.exp(sk - m))
        o = (_dot(p_loc.astype(BF16), vl) + _dot(p_ctx.astype(BF16), vx_ref[kk].astype(BF16))) / den
        for g in range(grp):
            o_ref[:, (q0 + g) * HEAD_DIM:(q0 + g + 1) * HEAD_DIM] = o[g * blk:(g + 1) * blk].astype(o_ref.dtype)


def swa_attention(proj, k_ctx, v_ctx, sink, cos2, sin2, *, row0, nseq, seq, q_col, k_col, v_col, n_kv, grp):
    blk = SWA_BLOCK
    nb = seq // blk
    rb0 = row0 // blk
    past = k_ctx.shape[2]
    kw = SWA_STEP_KV * HEAD_DIM
    qw = grp * kw
    qc, kc, vc = q_col // qw, k_col // kw, v_col // kw

    def kvspec(col, delta):
        return pl.BlockSpec((blk, kw),
                            lambda h, b, i: (rb0 + b * nb + jnp.clip(i + delta, 0, nb - 1), col + h))

    cache = pl.BlockSpec((None, SWA_STEP_KV, past, HEAD_DIM), lambda h, b, i: (b, h, 0, 0))
    table = pl.BlockSpec((seq, HEAD_DIM), lambda h, b, i: (0, 0))
    kern = functools.partial(_swa_kernel, grp=grp, nb=nb)
    return pl.pallas_call(
        kern,
        grid=(n_kv // SWA_STEP_KV, nseq, nb),
        in_specs=[pl.BlockSpec(memory_space=pltpu.SMEM),
                  pl.BlockSpec((blk, qw), lambda h, b, i: (rb0 + b * nb + i, qc + h)),
                  kvspec(kc, -1), kvspec(kc, 0), kvspec(kc, 1),
                  kvspec(vc, -1), kvspec(vc, 0), kvspec(vc, 1),
                  cache, cache, table, table],
        out_specs=pl.BlockSpec((blk, qw), lambda h, b, i: (b * nb + i, h)),
        out_shape=jax.ShapeDtypeStruct((nseq * seq, n_kv * grp * HEAD_DIM), BF16),
        compiler_params=_params("parallel", "parallel", "parallel"),
        name="swa_attention",
    )(sink, proj, proj, proj, proj, proj, proj, proj, k_ctx, v_ctx, cos2, sin2)


CONV_CHUNK = 256
CONV_PAD = 8


def _conv_silu_kernel(x_ref, w_ref, b_ref, o_ref, xpad, *, seq):
    chunk = min(CONV_CHUNK, seq)
    width = x_ref.shape[1]
    xpad[0:CONV_PAD, :] = jnp.zeros((CONV_PAD, width), F32)
    xpad[CONV_PAD + seq:, :] = jnp.zeros((CONV_PAD, width), F32)
    xpad[CONV_PAD:CONV_PAD + seq, :] = x_ref[...]
    win_rows = chunk + 2 * CONV_PAD

    def body(c, carry):
        t0 = pl.multiple_of(c * chunk, chunk)
        win = xpad[pl.ds(t0, win_rows), :]

        def at(off):
            return pltpu.roll(win, (win_rows - off) % win_rows, 0)[CONV_PAD:CONV_PAD + chunk]

        y = (w_ref[0:1, :] * at(-2) + w_ref[1:2, :] * at(-1) + w_ref[2:3, :] * at(0)
             + w_ref[3:4, :] * at(1) + b_ref[...])
        o_ref[pl.ds(t0, chunk), :] = _silu(y)
        return carry

    lax.fori_loop(0, seq // chunk, body, 0)


def conv_silu(proj, w, b, *, row0, nseq, seq, col0):
    width = w.shape[1]
    tc = 256
    rb0 = row0 // seq
    return pl.pallas_call(
        functools.partial(_conv_silu_kernel, seq=seq),
        grid=(nseq, width // tc),
        in_specs=[pl.BlockSpec((seq, tc), lambda b, j: (rb0 + b, col0 // tc + j)),
                  pl.BlockSpec((4, tc), lambda b, j: (0, j)),
                  pl.BlockSpec((1, tc), lambda b, j: (0, j))],
        out_specs=pl.BlockSpec((seq, tc), lambda b, j: (b, j)),
        out_shape=jax.ShapeDtypeStruct((nseq * seq, width), F32),
        scratch_shapes=[pltpu.VMEM((seq + 2 * CONV_PAD, tc), F32)],
        compiler_params=_params("parallel", "parallel"),
        name="conv_silu",
    )(proj, w, b.reshape(1, width))


SSD_STEP_GROUPS = 4


def _ssd_kernel(x_ref, b_ref, c_ref, dtc_ref, dtr_ref, plane_ref, psub_ref, h0_ref,
                y_ref, st_ref, ht, xw, *, rev, nchunks, n_state):
    c = pl.program_id(2)
    q = SSD_CHUNK
    gh = SSD_GROUP_HEADS
    gw = gh * SSD_HEADDIM
    hi = lax.Precision.HIGHEST

    @pl.when(c == 0)
    def _():
        ht[...] = h0_ref[...]

    ii = lax.broadcasted_iota(jnp.int32, (q, q), 0)
    jj = lax.broadcasted_iota(jnp.int32, (q, q), 1)
    tri = (jj >= ii) if rev else (jj <= ii)
    tri_f = tri.astype(F32)
    lane = lax.broadcasted_iota(jnp.int32, (q, 2 * SSD_HEADDIM), 1)
    lane1 = lane[0:1, :]
    last = 0 if rev else q - 1

    def pair(cols, p, lanes):
        rows = cols.shape[0]
        lo = jnp.broadcast_to(cols[:, 2 * p:2 * p + 1], (rows, 2 * SSD_HEADDIM))
        hi_ = jnp.broadcast_to(cols[:, 2 * p + 1:2 * p + 2], (rows, 2 * SSD_HEADDIM))
        return jnp.where(lanes < SSD_HEADDIM, lo, hi_)

    for gg in range(SSD_STEP_GROUPS):
        g = pl.program_id(1) * SSD_STEP_GROUPS + gg
        dtc = pltpu.roll(dtc_ref[...], lax.rem((128 // gh - g) * gh, 128), 1)
        a_lane = -jnp.exp(plane_ref[gg, 0:1, :])
        a_sub = -jnp.exp(psub_ref[gg, :, 0:1])
        dt_c = _softplus(dtc + plane_ref[gg, 1:2, :])
        dt_r = _softplus(dtr_ref[pl.ds(pl.multiple_of(g * gh, gh), gh), :] + psub_ref[gg, :, 1:2])
        acs_c = jnp.dot(tri_f, dt_c * a_lane, precision=hi, preferred_element_type=F32)
        acs_r = lax.dot_general(dt_r * a_sub, tri_f, NT_DIMS, precision=hi, preferred_element_type=F32)
        acs_last = acs_c[last:last + 1, :]
        to_end = jnp.exp(acs_last - acs_c) * dt_c
        eac = jnp.exp(acs_c)
        chunk_decay = jnp.exp(acs_last)
        bm = b_ref[:, gg * n_state:(gg + 1) * n_state].astype(BF16)
        cm = c_ref[:, gg * n_state:(gg + 1) * n_state].astype(BF16)
        cb = _dot_nt(cm, bm)
        y_off = _dot(cm, ht[gg].astype(BF16))
        decay_parts = []
        for p in range(gh // 2):
            lo_col = gg * gw + p * 2 * SSD_HEADDIM
            cols = slice(lo_col, lo_col + 2 * SSD_HEADDIM)
            pcols = slice(p * 2 * SSD_HEADDIM, (p + 1) * 2 * SSD_HEADDIM)
            xp = x_ref[:, cols]
            yp = y_off[:, pcols] * pair(eac, p, lane)
            for half in range(2):
                h = 2 * p + half
                seg = jnp.where(tri, acs_c[:, h:h + 1] - acs_r[h:h + 1, :], MASKED)
                w = cb * jnp.exp(seg) * dt_r[h:h + 1, :]
                in_half = (lane >= SSD_HEADDIM) if half else (lane < SSD_HEADDIM)
                yp = yp + _dot(w.astype(BF16), jnp.where(in_half, xp, 0.0).astype(BF16))
            y_ref[:, cols] = yp
            xw[:, cols] = (xp * pair(to_end, p, lane)).astype(BF16)
            decay_parts.append(pair(chunk_decay, p, lane1))
        st = lax.dot_general(bm, xw[:, gg * gw:(gg + 1) * gw], TN_DIMS, preferred_element_type=F32)
        ht[gg] = ht[gg] * jnp.concatenate(decay_parts, axis=1) + st

    @pl.when(c == nchunks - 1)
    def _():
        for gg in range(SSD_STEP_GROUPS):
            st_ref[gg] = ht[gg].T


def ssd_scan(xbc, dt_col, dt_row, plane, psub, h0t, *, row0, nseq, seq, rev, d_ssd, n_groups, d_state):
    q = SSD_CHUNK
    nchunks = seq // q
    width = SSD_GROUP_HEADS * SSD_HEADDIM
    rb0 = row0 // q
    bcol = d_ssd // d_state
    ccol = bcol + n_groups

    def chunk(c):
        return (nchunks - 1 - c) if rev else c

    sg = SSD_STEP_GROUPS
    assert n_groups % sg == 0 and bcol % sg == 0 and ccol % sg == 0
    kern = functools.partial(_ssd_kernel, rev=rev, nchunks=nchunks, n_state=d_state)
    return pl.pallas_call(
        kern,
        grid=(nseq, n_groups // sg, nchunks),
        in_specs=[
            pl.BlockSpec((q, sg * width), lambda b, g, c: (b * nchunks + chunk(c), g)),
            pl.BlockSpec((q, sg * d_state), lambda b, g, c: (b * nchunks + chunk(c), bcol // sg + g)),
            pl.BlockSpec((q, sg * d_state), lambda b, g, c: (b * nchunks + chunk(c), ccol // sg + g)),
            pl.BlockSpec((q, 128), lambda b, g, c: (rb0 + b * nchunks + chunk(c), 0)),
            pl.BlockSpec((n_groups * SSD_GROUP_HEADS, q), lambda b, g, c: (0, rb0 + b * nchunks + chunk(c))),
            pl.BlockSpec((sg, 8, 128), lambda b, g, c: (g, 0, 0)),
            pl.BlockSpec((sg, 8, 128), lambda b, g, c: (g, 0, 0)),
            pl.BlockSpec((None, sg, d_state, width), lambda b, g, c: (b, g, 0, 0)),
        ],
        out_specs=[
            pl.BlockSpec((q, sg * width), lambda b, g, c: (b * nchunks + chunk(c), g)),
            pl.BlockSpec((None, sg, width, d_state), lambda b, g, c: (b, g, 0, 0)),
        ],
        out_shape=[jax.ShapeDtypeStruct((nseq * seq, d_ssd), F32),
                   jax.ShapeDtypeStruct((nseq, n_groups, width, d_state), F32)],
        scratch_shapes=[pltpu.VMEM((sg, d_state, width), F32), pltpu.VMEM((q, sg * width), BF16)],
        compiler_params=_params("parallel", "parallel", "arbitrary"),
        name="ssd_scan",
    )(xbc, xbc, xbc, dt_col, dt_row, plane, psub, h0t)


def _ssd_out_kernel(yf_ref, yb_ref, x_ref, d_ref, g_ref, *rest):
    z_refs, o_ref = rest[:-1], rest[-1]
    z = jnp.concatenate([r[...] for r in z_refs], axis=1)
    y = yf_ref[...] + yb_ref[...] + d_ref[...] * x_ref[...]
    o_ref[...] = _rms(y * _silu(z), g_ref[...]).astype(o_ref.dtype)


def ssd_out(yf, yb, xbc, proj, d_vec, norm_g, *, row0, z_col):
    m, d_ssd = yf.shape
    tm = EW_ROWS
    rb0 = row0 // tm
    zw = math.gcd(z_col, d_ssd)
    row = pl.BlockSpec((tm, d_ssd), lambda i: (i, 0))
    vec = pl.BlockSpec((1, d_ssd), lambda i: (0, 0))
    z_specs = [pl.BlockSpec((tm, zw), lambda i, p=p: (rb0 + i, z_col // zw + p)) for p in range(d_ssd // zw)]
    return pl.pallas_call(
        _ssd_out_kernel,
        grid=(m // tm,),
        in_specs=[row, row, row, vec, vec] + z_specs,
        out_specs=row,
        out_shape=jax.ShapeDtypeStruct((m, d_ssd), BF16),
        compiler_params=_params("parallel"),
        name="ssd_out",
    )(yf, yb, xbc, d_vec.reshape(1, d_ssd), norm_g.reshape(1, d_ssd), *([proj] * (d_ssd // zw)))


def ssd_params(a_log, dt_bias, n_groups):
    gh = SSD_GROUP_HEADS
    al = a_log.reshape(2, n_groups, gh)
    db = dt_bias.reshape(2, n_groups, gh)
    plane = jnp.zeros((2, n_groups, 8, 128), F32).at[:, :, 0, :gh].set(al).at[:, :, 1, :gh].set(db)
    psub = jnp.zeros((2, n_groups, 8, 128), F32).at[:, :, :, 0].set(al).at[:, :, :, 1].set(db)
    return plane, psub


def ssd_mixer(proj, dt_col, dt_row, h0, conv_w, conv_b, a_log, dt_bias, d_skip, norm_g, *,
              row0, nseq, seq, z_col, xbc_col, d_ssd, n_groups, d_state):
    width = SSD_GROUP_HEADS * SSD_HEADDIM
    xbc = conv_silu(proj, conv_w, conv_b, row0=row0, nseq=nseq, seq=seq, col0=xbc_col)
    plane, psub = ssd_params(a_log, dt_bias, n_groups)
    if h0 is None:
        h0t = jnp.zeros((2, nseq, n_groups, d_state, width), F32)
    else:
        h0t = h0.astype(F32).reshape(nseq, 2, n_groups, width, d_state).transpose(1, 0, 2, 4, 3)
    ys, sts = [], []
    for d in range(2):
        y, st = ssd_scan(xbc, dt_col, dt_row, plane[d], psub[d], h0t[d], row0=row0, nseq=nseq, seq=seq,
                         rev=bool(d), d_ssd=d_ssd, n_groups=n_groups, d_state=d_state)
        ys.append(y)
        sts.append(st.reshape(nseq, n_groups * SSD_GROUP_HEADS, SSD_HEADDIM, d_state))
    d_vec = jnp.repeat(d_skip.astype(F32), SSD_HEADDIM)
    y = ssd_out(ys[0], ys[1], xbc, proj, d_vec, norm_g, row0=row0, z_col=z_col)
    return y, jnp.stack(sts, axis=1)


def kernel(x_prompt, x_sample, cache_na_k, cache_na_v, state_rglru, cache_swa_k, cache_swa_v, state_ssd, c, c_ctx, w_mod, b_mod, norm_mix_pre, norm_mix_post, norm_ffn_pre, norm_ffn_post, w_in_even, rg_conv_w, rg_conv_b, rg_w_a, rg_b_a, rg_w_i, rg_b_i, rg_lambda, na_rel_bias, w_in_odd, swa_sink, ssd_conv_w, ssd_conv_b, ssd_a_log, ssd_dt_bias, ssd_d, ssd_norm, w_out, ffn_w_up, ffn_conv_w, ffn_conv_b, ffn_w_down):
    nb_ctx, ctx_len, d = x_prompt.shape
    nb_lat, lat_len, _ = x_sample.shape
    depth = w_mod.shape[0]
    n_ctx = nb_ctx * ctx_len
    n_lat = nb_lat * lat_len
    d_a = rg_conv_w.shape[2]
    n_heads_b = na_rel_bias.shape[1]
    d_b = n_heads_b * HEAD_DIM
    n_heads_c = swa_sink.shape[1]
    n_kv_c = cache_swa_k.shape[2]
    grp_c = n_heads_c // n_kv_c
    d_c = n_heads_c * HEAD_DIM
    d_kv = n_kv_c * HEAD_DIM
    n_ssd_heads = ssd_d.shape[1]
    d_ssd = n_ssd_heads * SSD_HEADDIM
    d_state = state_ssd.shape[-1]
    n_groups = n_ssd_heads // SSD_GROUP_HEADS
    d_xbc = ssd_conv_w.shape[2]
    d_ff = ffn_w_down.shape[1]
    d_ff_pad = -(-d_ff // 1024) * 1024
    shapes = dict(n_ctx=n_ctx, lat_len=lat_len)

    n_even, n_odd = (depth + 1) // 2, depth // 2

    x_pair = (x_prompt.reshape(n_ctx, d), x_sample.reshape(n_lat, d))
    c8 = jnp.zeros((MOD_ROWS, d), F32).at[0].set(c_ctx).at[1:1 + nb_lat].set(c)
    mod = modulation_all(c8, w_mod, b_mod)
    cos2, sin2 = rope_tables(lat_len)
    w_even = w_in_even.astype(BF16)
    w_odd = w_in_odd.astype(BF16)
    w_o = w_out.astype(BF16)
    w_up = cast_split_pad_cols(ffn_w_up, d_ff_pad)
    w_down = cast_pad_rows(ffn_w_down, d_ff_pad)
    conv = jnp.concatenate([ffn_conv_w, ffn_conv_b[:, None]], axis=1)
    conv = conv.reshape(depth, 4, 2, d_ff).transpose(0, 2, 1, 3)
    conv = jnp.pad(conv, ((0, 0), (0, 0), (0, 4), (0, d_ff_pad - d_ff)))
    rg_wa, rg_wi = rg_w_a.astype(BF16), rg_w_i.astype(BF16)

    def pad_cols(a, n):
        return jnp.pad(a, ((0, 0), (0, n - a.shape[1])))

    na_caches = swa_caches = None
    new_rglru, new_ssd = [], []
    h = adaln(x_pair, (mod, 0), norm_mix_pre[0], shift_i=0, scale_i=1, **shapes)
    for layer in range(depth):
        if layer % 2 == 0:
            e = layer // 2
            proj = matmul(h, w_even, e)
            rg = (rg_conv_w[e], rg_conv_b[e], rg_wa[e], rg_b_a[e], rg_wi[e], rg_b_i[e], rg_lambda[e])
            ya_c, st = rglru_mixer(proj, jnp.zeros((nb_ctx, 2, d_a), F32), *rg,
                                   row0=0, nseq=nb_ctx, seq=ctx_len, d_a=d_a, heads=4)
            ya_l, _ = rglru_mixer(proj, state_rglru[:, e].astype(F32), *rg,
                                  row0=n_ctx, nseq=nb_lat, seq=lat_len, d_a=d_a, heads=1)
            cols = dict(q_col=2 * d_a, k_col=2 * d_a + d_b, v_col=2 * d_a + 2 * d_b)
            yb_c, *na_caches = ctx_attention(proj, jnp.zeros((n_heads_b,), F32), na_caches, e, n_even,
                                             nseq=nb_ctx, seq=ctx_len, n_kv=n_heads_b, grp=1, use_sink=False,
                                             **cols)
            yb_l = na_attention(proj, cache_na_k[:, e], cache_na_v[:, e], na_bias_table(na_rel_bias[e]),
                                row0=n_ctx, nseq=nb_lat, seq=lat_len, n_heads=n_heads_b, **cols)
            new_rglru.append(st)
            mix_c, mix_l = (ya_c, yb_c), (ya_l, yb_l)
        else:
            od = layer // 2
            n_main = d_c + 2 * d_kv + d_ssd + d_xbc
            proj = matmul(h, w_odd, od, n=n_main)
            w_dt = w_odd[od][:, n_main:]
            dt_col, dt_row = dt_proj(h, pad_cols(w_dt, 128), w_dt.T)
            cols = dict(q_col=0, k_col=d_c, v_col=d_c + d_kv)
            yc_c, *swa_caches = ctx_attention(proj, swa_sink[od], swa_caches, od, n_odd,
                                              nseq=nb_ctx, seq=ctx_len, n_kv=n_kv_c, grp=grp_c, use_sink=True,
                                              **cols)
            yc_l = swa_attention(proj, cache_swa_k[:, od], cache_swa_v[:, od], swa_sink[od], cos2, sin2,
                                 row0=n_ctx, nseq=nb_lat, seq=lat_len, n_kv=n_kv_c, grp=grp_c, **cols)
            ssd = (ssd_conv_w[od], ssd_conv_b[od], ssd_a_log[od], ssd_dt_bias[od], ssd_d[od], ssd_norm[od])
            scols = dict(z_col=d_c + 2 * d_kv, xbc_col=d_c + 2 * d_kv + d_ssd, d_ssd=d_ssd,
                         n_groups=n_groups, d_state=d_state)
            yd_c, st = ssd_mixer(proj, dt_col, dt_row, None, *ssd, row0=0, nseq=nb_ctx, seq=ctx_len, **scols)
            yd_l, _ = ssd_mixer(proj, dt_col, dt_row, state_ssd[:, od], *ssd,
                                row0=n_ctx, nseq=nb_lat, seq=lat_len, **scols)
            new_ssd.append(st)
            mix_c, mix_l = (yc_c, yd_c), (yc_l, yd_l)
        y_pair = (matmul_pair(*mix_c, w_o, layer), matmul_pair(*mix_l, w_o, layer))
        *x_pair, h = resid_adaln(x_pair, y_pair, (mod, layer), norm_mix_post[layer], norm_ffn_pre[layer],
                                 (mod, layer), gate_i=2, shift_i=3, scale_i=4, **shapes)
        t = ffn_up(h, w_up, conv, layer, n_ctx=n_ctx, ctx_len=ctx_len, lat_len=lat_len)
        f = matmul_ktiled(t, w_down, layer, tk=d_ff_pad // 4)
        nxt = min(layer + 1, depth - 1)
        *x_pair, h = resid_adaln(x_pair, (f, f), (mod, layer), norm_ffn_post[layer], norm_mix_pre[nxt],
                                 (mod, nxt), gate_i=5, shift_i=0, scale_i=1, **shapes)
    return (x_pair[0].reshape(nb_ctx, ctx_len, d), x_pair[1].reshape(nb_lat, lat_len, d),
            na_caches[0], na_caches[1], jnp.stack(new_rglru, axis=1),
            swa_caches[0], swa_caches[1], jnp.stack(new_ssd, axis=1))
```

```python
import functools
import math

import jax
import jax.numpy as jnp
from jax import lax
from jax.experimental import pallas as pl
from jax.experimental.pallas import tpu as pltpu

F32 = jnp.float32
BF16 = jnp.bfloat16

RMS_EPS = 1e-6
HEAD_DIM = 128
GRID_W = 64
NA_ROWS = 8
NA_COLS = 16
NA_QROWS = 4
SWA_BLOCK = 128
SWA_WINDOW = 128
ROPE_BASE = 10000.0
RG_C = 8.0
SSD_CHUNK = 128
SSD_HEADDIM = 64
SSD_GROUP_HEADS = 8
MASKED = -1e30
MOD_ROWS = 8
V7X_VMEM_BYTES = 64 * 1024 * 1024
VMEM_LIMIT = V7X_VMEM_BYTES - 8 * 1024 * 1024
ROW_TILE = 1024
COL_TILE = 1024
EW_ROWS = 256
NT_DIMS = (((1,), (1,)), ((), ()))
TN_DIMS = (((0,), (0,)), ((), ()))


def _params(*sem):
    return pltpu.CompilerParams(dimension_semantics=sem, vmem_limit_bytes=VMEM_LIMIT)


def _sigmoid(x):
    return 0.5 * jnp.tanh(0.5 * x) + 0.5


def _silu(x):
    return x * _sigmoid(x)


def _softplus(x):
    return jnp.maximum(x, 0.0) + jnp.log(1.0 + jnp.exp(-jnp.abs(x)))


def _gelu_tanh(x):
    return 0.5 * x * (1.0 + jnp.tanh(math.sqrt(2.0 / math.pi) * (x + 0.044715 * (x * x * x))))


def _rms(x, g):
    ms = jnp.mean(x * x, axis=-1, keepdims=True)
    return x * lax.rsqrt(ms + RMS_EPS) * g


def _dot(a, b):
    return jnp.dot(a, b, preferred_element_type=F32)


def _dot_nt(a, b):
    return lax.dot_general(a, b, NT_DIMS, preferred_element_type=F32)


def _mod_kernel(c_ref, w_ref, b_ref, o_ref):
    s = _silu(c_ref[...])
    o_ref[...] = _dot(s.astype(BF16), w_ref[...].astype(BF16)) + b_ref[...]


def modulation_all(c8, w_mod, b_mod):
    depth, d, n6 = w_mod.shape
    tn = 512
    per = d // tn
    return pl.pallas_call(
        _mod_kernel,
        grid=(depth, n6 // tn),
        in_specs=[
            pl.BlockSpec((MOD_ROWS, d), lambda l, j: (0, 0)),
            pl.BlockSpec((None, d, tn), lambda l, j: (l, 0, j)),
            pl.BlockSpec((None, 1, tn), lambda l, j: (l, 0, j)),
        ],
        out_specs=pl.BlockSpec((None, None, MOD_ROWS, tn), lambda l, j: (l, j // per, 0, j % per)),
        out_shape=jax.ShapeDtypeStruct((depth, 6, MOD_ROWS, d), F32),
        compiler_params=_params("parallel", "parallel"),
        name="modulation",
    )(c8, w_mod, b_mod.reshape(depth, 1, n6))


def _mod_row(i, tm, n_ctx, lat_len):
    row0 = i * tm
    return jnp.where(row0 < n_ctx, 0, 1 + (row0 - n_ctx) // lat_len)


def _group_specs(pair, tm, d, nc):
    lat0 = nc if pair[0] is pair[1] else 0
    return [pl.BlockSpec((tm, d), lambda i: (jnp.minimum(i, nc - 1), 0)),
            pl.BlockSpec((tm, d), lambda i: (lat0 + jnp.maximum(i - nc, 0), 0))]


def _adaln_kernel(xc_ref, xl_ref, mod_ref, g_ref, h_ref, *, tm, n_ctx, lat_len, shift_i, scale_i):
    i = pl.program_id(0)
    r = _mod_row(i, tm, n_ctx, lat_len)
    x = jnp.where(i * tm < n_ctx, xc_ref[...], xl_ref[...])
    shift = mod_ref[shift_i, pl.ds(r, 1), :]
    scale = mod_ref[scale_i, pl.ds(r, 1), :]
    h_ref[...] = (_rms(x, g_ref[...]) * (1.0 + scale) + shift).astype(h_ref.dtype)


def adaln(x_pair, mod, g, *, n_ctx, lat_len, shift_i, scale_i):
    d = x_pair[0].shape[1]
    tm = EW_ROWS
    nc = n_ctx // tm
    m = n_ctx + x_pair[1].shape[0] - (n_ctx if x_pair[0] is x_pair[1] else 0)
    kern = functools.partial(_adaln_kernel, tm=tm, n_ctx=n_ctx, lat_len=lat_len,
                             shift_i=shift_i, scale_i=scale_i)
    return pl.pallas_call(
        kern,
        grid=(m // tm,),
        in_specs=_group_specs(x_pair, tm, d, nc) + [
            pl.BlockSpec((None, 6, MOD_ROWS, d), lambda i: (mod[1], 0, 0, 0)),
            pl.BlockSpec((1, d), lambda i: (0, 0)),
        ],
        out_specs=pl.BlockSpec((tm, d), lambda i: (i, 0)),
        out_shape=jax.ShapeDtypeStruct((m, d), BF16),
        compiler_params=_params("parallel"),
        name="adaln",
    )(*x_pair, mod[0], g.reshape(1, d))


def _resid_kernel(x_ref, y_ref, mod_ref, gpost_ref, gpre_ref, modn_ref, *rest,
                  tm, latent, lat_len, gate_i, shift_i, scale_i):
    xo_ref, h_ref = rest[-2:]
    r = 1 + (pl.program_id(0) * tm) // lat_len if latent else 0
    gate = mod_ref[gate_i, pl.ds(r, 1), :]
    x = x_ref[...] + gate * _rms(y_ref[...], gpost_ref[...])
    xo_ref[...] = x
    shift = modn_ref[shift_i, pl.ds(r, 1), :]
    scale = modn_ref[scale_i, pl.ds(r, 1), :]
    h_ref[...] = (_rms(x, gpre_ref[...]) * (1.0 + scale) + shift).astype(h_ref.dtype)


def _resid_group(x, y, y_row0, h_prev, h_row0, m, mod, gpost, gpre, modn, *, latent, lat_len,
                 gate_i, shift_i, scale_i):
    rows, d = x.shape
    tm = EW_ROWS
    y_off, h_off = y_row0 // tm, h_row0 // tm
    kern = functools.partial(_resid_kernel, tm=tm, latent=latent, lat_len=lat_len,
                             gate_i=gate_i, shift_i=shift_i, scale_i=scale_i)
    vec = pl.BlockSpec((1, d), lambda i: (0, 0))

    def modspec(layer):
        return pl.BlockSpec((None, 6, MOD_ROWS, d), lambda i: (layer, 0, 0, 0))

    prev = [] if h_prev is None else [h_prev]
    return pl.pallas_call(
        kern,
        grid=(rows // tm,),
        in_specs=[pl.BlockSpec((tm, d), lambda i: (i, 0)),
                  pl.BlockSpec((tm, d), lambda i: (i + y_off, 0)),
                  modspec(mod[1]), vec, vec, modspec(modn[1])] + [pl.BlockSpec(memory_space=pl.ANY)] * len(prev),
        out_specs=[pl.BlockSpec((tm, d), lambda i: (i, 0)),
                   pl.BlockSpec((tm, d), lambda i: (i + h_off, 0))],
        out_shape=[jax.ShapeDtypeStruct((rows, d), F32), jax.ShapeDtypeStruct((m, d), BF16)],
        input_output_aliases={6: 1} if prev else {},
        compiler_params=_params("parallel"),
        name="resid_adaln",
    )(x, y, mod[0], gpost.reshape(1, d), gpre.reshape(1, d), modn[0], *prev)


def resid_adaln(x_pair, y_pair, mod, gpost, gpre, modn, *, n_ctx, lat_len, gate_i, shift_i, scale_i):
    n_lat = x_pair[1].shape[0]
    m = n_ctx + n_lat
    lat_row0 = n_ctx if y_pair[0] is y_pair[1] else 0
    common = dict(lat_len=lat_len, gate_i=gate_i, shift_i=shift_i, scale_i=scale_i)
    xc, h = _resid_group(x_pair[0], y_pair[0], 0, None, 0, m, mod, gpost, gpre, modn, latent=False, **common)
    xl, h = _resid_group(x_pair[1], y_pair[1], lat_row0, h, n_ctx, m, mod, gpost, gpre, modn, latent=True,
                         **common)
    return xc, xl, h


def _mm_kernel(a_ref, w_ref, o_ref):
    o_ref[...] = _dot(a_ref[...], w_ref[...]).astype(o_ref.dtype)


def matmul(a, w, layer, n=None, out_dtype=F32):
    m, k = a.shape
    n = w.shape[2] if n is None else n
    tm, tn = min(ROW_TILE, m), min(COL_TILE, n)
    return pl.pallas_call(
        _mm_kernel,
        grid=(m // tm, n // tn),
        in_specs=[pl.BlockSpec((tm, k), lambda i, j: (i, 0)),
                  pl.BlockSpec((None, k, tn), lambda i, j: (layer, 0, j))],
        out_specs=pl.BlockSpec((tm, tn), lambda i, j: (i, j)),
        out_shape=jax.ShapeDtypeStruct((m, n), out_dtype),
        compiler_params=_params("parallel", "parallel"),
        name="matmul",
    )(a, w)


def _mm_pair_kernel(a1_ref, a2_ref, w1_ref, w2_ref, o_ref):
    o_ref[...] = _dot(a1_ref[...], w1_ref[...]) + _dot(a2_ref[...], w2_ref[...])


def matmul_pair(a1, a2, w, layer):
    m, k = a1.shape
    n = w.shape[2]
    assert a2.shape == a1.shape and w.shape[1] == 2 * k
    tm, tn = min(ROW_TILE, m), min(COL_TILE, n)
    return pl.pallas_call(
        _mm_pair_kernel,
        grid=(m // tm, n // tn),
        in_specs=[pl.BlockSpec((tm, k), lambda i, j: (i, 0)),
                  pl.BlockSpec((tm, k), lambda i, j: (i, 0)),
                  pl.BlockSpec((None, k, tn), lambda i, j: (layer, 0, j)),
                  pl.BlockSpec((None, k, tn), lambda i, j: (layer, 1, j))],
        out_specs=pl.BlockSpec((tm, tn), lambda i, j: (i, j)),
        out_shape=jax.ShapeDtypeStruct((m, n), F32),
        compiler_params=_params("parallel", "parallel"),
        name="matmul_pair",
    )(a1, a2, w, w)


def _mm_acc_kernel(a_ref, w_ref, o_ref):
    @pl.when(pl.program_id(2) == 0)
    def _():
        o_ref[...] = jnp.zeros_like(o_ref)

    o_ref[...] += _dot(a_ref[...], w_ref[...])


def matmul_ktiled(a, w, layer, tk):
    m, k = a.shape
    n = w.shape[2]
    tm, tn = min(ROW_TILE, m), min(COL_TILE, n)
    return pl.pallas_call(
        _mm_acc_kernel,
        grid=(m // tm, n // tn, k // tk),
        in_specs=[pl.BlockSpec((tm, tk), lambda i, j, kk: (i, kk)),
                  pl.BlockSpec((None, tk, tn), lambda i, j, kk: (layer, kk, j))],
        out_specs=pl.BlockSpec((tm, tn), lambda i, j, kk: (i, j)),
        out_shape=jax.ShapeDtypeStruct((m, n), F32),
        compiler_params=_params("parallel", "parallel", "arbitrary"),
        name="matmul_ktiled",
    )(a, w)


CAST_TILE = 256


def _cast_pad_kernel(w_ref, o_ref, *, n_real, axis):
    @pl.when(pl.program_id(axis) < n_real)
    def _():
        o_ref[...] = w_ref[...].astype(o_ref.dtype)

    @pl.when(pl.program_id(axis) >= n_real)
    def _():
        o_ref[...] = jnp.zeros_like(o_ref)


def cast_split_pad_cols(w, n_pad):
    depth, k, n2 = w.shape
    n = n2 // 2
    t = CAST_TILE
    assert n % t == 0 and n_pad % t == 0
    n_real = n // t
    kern = functools.partial(_cast_pad_kernel, n_real=n_real, axis=2)
    return pl.pallas_call(
        kern,
        grid=(depth, 2, n_pad // t),
        in_specs=[pl.BlockSpec((None, k, t), lambda l, s, j: (l, 0, s * n_real + jnp.minimum(j, n_real - 1)))],
        out_specs=pl.BlockSpec((None, None, k, t), lambda l, s, j: (l, s, 0, j)),
        out_shape=jax.ShapeDtypeStruct((depth, 2, k, n_pad), BF16),
        compiler_params=_params("parallel", "parallel", "parallel"),
        name="cast_split_pad_cols",
    )(w)


def cast_pad_rows(w, k_pad):
    depth, k, n = w.shape
    t = CAST_TILE
    assert k % t == 0 and k_pad % t == 0
    n_real = k // t
    kern = functools.partial(_cast_pad_kernel, n_real=n_real, axis=1)
    return pl.pallas_call(
        kern,
        grid=(depth, k_pad // t),
        in_specs=[pl.BlockSpec((None, t, n), lambda l, j: (l, jnp.minimum(j, n_real - 1), 0))],
        out_specs=pl.BlockSpec((None, t, n), lambda l, j: (l, j, 0)),
        out_shape=jax.ShapeDtypeStruct((depth, k_pad, n), BF16),
        compiler_params=_params("parallel", "parallel"),
        name="cast_pad_rows",
    )(w)


def _dt_kernel(h_ref, w_ref, wt_ref, oc_ref, or_ref):
    h = h_ref[...]
    oc_ref[...] = _dot(h, w_ref[...])
    or_ref[...] = _dot_nt(wt_ref[...], h)


def dt_proj(h, w_pad, w_t):
    m, k = h.shape
    nh = w_t.shape[0]
    tm = min(ROW_TILE, m)
    return pl.pallas_call(
        _dt_kernel,
        grid=(m // tm,),
        in_specs=[pl.BlockSpec((tm, k), lambda i: (i, 0)),
                  pl.BlockSpec((k, 128), lambda i: (0, 0)),
                  pl.BlockSpec((nh, k), lambda i: (0, 0))],
        out_specs=[pl.BlockSpec((tm, 128), lambda i: (i, 0)),
                   pl.BlockSpec((nh, tm), lambda i: (0, i))],
        out_shape=[jax.ShapeDtypeStruct((m, 128), F32), jax.ShapeDtypeStruct((nh, m), F32)],
        compiler_params=_params("parallel"),
        name="dt_proj",
    )(h, w_pad, w_t)


FFN_HALO = 16
FFN_PARTS = 1


def _ffn_up_kernel(a_ref, ap_ref, an_ref, wg_ref, wv_ref, cg_ref, cv_ref, o_ref, aext, *,
                   tm, n_ctx, ctx_len, lat_len):
    i = pl.program_id(0)

    @pl.when(pl.program_id(1) == 0)
    def _():
        aext[0:FFN_HALO, :] = ap_ref[...]
        aext[FFN_HALO:FFN_HALO + tm, :] = a_ref[...]
        aext[FFN_HALO + tm:, :] = an_ref[...]

    a = aext[...]
    local = lax.broadcasted_iota(jnp.int32, (tm, 1), 0)
    is_ctx = i * tm < n_ctx
    pos = jnp.where(is_ctx, lax.rem(local, ctx_len), lax.rem(i * tm - n_ctx, lat_len) + local)
    has_prev = pos != 0
    has_next = pos != jnp.where(is_ctx, ctx_len - 1, lat_len - 1)
    rows = tm + 2 * FFN_HALO

    def conv(u, c_ref, cols):
        up = pltpu.roll(u, 1, 0)[FFN_HALO:FFN_HALO + tm]
        un = pltpu.roll(u, rows - 1, 0)[FFN_HALO:FFN_HALO + tm]
        uc = u[FFN_HALO:FFN_HALO + tm]
        return (c_ref[0:1, cols] * jnp.where(has_prev, up, 0.0) + c_ref[1:2, cols] * uc
                + c_ref[2:3, cols] * jnp.where(has_next, un, 0.0) + c_ref[3:4, cols])

    part = o_ref.shape[1] // FFN_PARTS
    for p in range(FFN_PARTS):
        cols = slice(p * part, (p + 1) * part)
        g = conv(_dot(a, wg_ref[:, cols]), cg_ref, cols)
        v = conv(_dot(a, wv_ref[:, cols]), cv_ref, cols)
        o_ref[:, cols] = (_silu(g) * v).astype(o_ref.dtype)


def ffn_up(h, w, conv, layer, *, n_ctx, ctx_len, lat_len):
    m, k = h.shape
    n = w.shape[3]
    tm = min(ROW_TILE, m)
    tn = 512
    assert tm % ctx_len == 0 and lat_len % tm == 0 and n_ctx % tm == 0
    nhalo = m // FFN_HALO
    per = tm // FFN_HALO
    kern = functools.partial(_ffn_up_kernel, tm=tm, n_ctx=n_ctx, ctx_len=ctx_len, lat_len=lat_len)
    return pl.pallas_call(
        kern,
        grid=(m // tm, n // tn),
        in_specs=[
            pl.BlockSpec((tm, k), lambda i, j: (i, 0)),
            pl.BlockSpec((FFN_HALO, k), lambda i, j: (jnp.maximum(i * per - 1, 0), 0)),
            pl.BlockSpec((FFN_HALO, k), lambda i, j: (jnp.minimum((i + 1) * per, nhalo - 1), 0)),
            pl.BlockSpec((None, None, k, tn), lambda i, j: (layer, 0, 0, j)),
            pl.BlockSpec((None, None, k, tn), lambda i, j: (layer, 1, 0, j)),
            pl.BlockSpec((None, None, 8, tn), lambda i, j: (layer, 0, 0, j)),
            pl.BlockSpec((None, None, 8, tn), lambda i, j: (layer, 1, 0, j)),
        ],
        out_specs=pl.BlockSpec((tm, tn), lambda i, j: (i, j)),
        out_shape=jax.ShapeDtypeStruct((m, n), BF16),
        scratch_shapes=[pltpu.VMEM((tm + 2 * FFN_HALO, k), BF16)],
        compiler_params=_params("parallel", "arbitrary"),
        name="ffn_up",
    )(h, h, h, w, w, conv, conv)


RG_CHUNK = 256
RG_PAD = 8


def _scan_chunk(a, u, h, reverse):
    nblk = a.shape[0] // 8
    row8 = lax.broadcasted_iota(jnp.int32, (8, a.shape[1]), 0)
    outs = [None] * nblk
    for b in (range(nblk - 1, -1, -1) if reverse else range(nblk)):
        aa = a[8 * b:8 * b + 8]
        uu = u[8 * b:8 * b + 8]
        for s in (1, 2, 4):
            keep = (row8 < 8 - s) if reverse else (row8 >= s)
            shift = 8 - s if reverse else s
            a_sh = jnp.where(keep, pltpu.roll(aa, shift, 0), 1.0)
            u_sh = jnp.where(keep, pltpu.roll(uu, shift, 0), 0.0)
            uu = aa * u_sh + uu
            aa = aa * a_sh
        hb = aa * h + uu
        outs[b] = hb
        h = hb[0:1] if reverse else hb[7:8]
    return jnp.concatenate(outs, axis=0), h


def _rglru_kernel(ga_ref, xa_ref, h0_ref, cw_ref, cb_ref, wa_ref, ba_ref, wi_ref, bi_ref, lam_ref,
                  y_ref, st_ref, xpad, hf, xcs, *, seq, heads):
    chunk = min(RG_CHUNK, seq)
    nchunks = seq // chunk
    width = heads * HEAD_DIM
    xpad[0:RG_PAD, :] = jnp.zeros((RG_PAD, width), F32)
    xpad[RG_PAD + seq:, :] = jnp.zeros((RG_PAD, width), F32)
    xpad[RG_PAD:RG_PAD + seq, :] = xa_ref[...]
    win_rows = chunk + 2 * RG_PAD

    for hh in range(heads):
        lanes = slice(hh * HEAD_DIM, (hh + 1) * HEAD_DIM)
        rate = [(-RG_C) * _softplus(-lam_ref[d:d + 1, lanes]) for d in range(2)]

        def conv(t0, lanes=lanes):
            win = xpad[pl.ds(t0, win_rows), lanes]

            def at(off):
                return pltpu.roll(win, (win_rows - off) % win_rows, 0)[RG_PAD:RG_PAD + chunk]

            return (cw_ref[0:1, lanes] * at(-2) + cw_ref[1:2, lanes] * at(-1)
                    + cw_ref[2:3, lanes] * at(0) + cw_ref[3:4, lanes] * at(1) + cb_ref[0:1, lanes])

        def coeffs(xc, d, lanes=lanes, hh=hh, rate=rate):
            xb = xc.astype(BF16)
            r = _sigmoid(_dot(xb, wa_ref[d, hh]) + ba_ref[d:d + 1, lanes])
            gate_i = _sigmoid(_dot(xb, wi_ref[d, hh]) + bi_ref[d:d + 1, lanes])
            a = jnp.exp(r * rate[d])
            u = jnp.sqrt(1.0 - a * a) * (gate_i * xc)
            return a, u

        def fwd(c, h, lanes=lanes, conv=conv, coeffs=coeffs):
            t0 = pl.multiple_of(c * chunk, chunk)
            xc = conv(t0)
            xcs[pl.ds(t0, chunk), lanes] = xc
            a, u = coeffs(xc, 0)
            hs, h = _scan_chunk(a, u, h, reverse=False)
            hf[pl.ds(t0, chunk), lanes] = hs
            return h

        h_f = lax.fori_loop(0, nchunks, fwd, h0_ref[0:1, lanes])

        def bwd(cc, h, lanes=lanes, coeffs=coeffs):
            t0 = pl.multiple_of((nchunks - 1 - cc) * chunk, chunk)
            a, u = coeffs(xcs[pl.ds(t0, chunk), lanes], 1)
            hs, h = _scan_chunk(a, u, h, reverse=True)
            y = (hf[pl.ds(t0, chunk), lanes] + hs) * _gelu_tanh(ga_ref[pl.ds(t0, chunk), lanes])
            y_ref[pl.ds(t0, chunk), lanes] = y.astype(y_ref.dtype)
            return h

        h_b = lax.fori_loop(0, nchunks, bwd, h0_ref[1:2, lanes])
        st_ref[0:1, lanes] = h_f
        st_ref[1:2, lanes] = h_b


def rglru_mixer(proj, h0, conv_w, conv_b, w_a, b_a, w_i, b_i, lam, *, row0, nseq, seq, d_a, heads):
    width = heads * HEAD_DIM
    ncol = d_a // width
    rb0 = row0 // seq
    kern = functools.partial(_rglru_kernel, seq=seq, heads=heads)
    vec2 = pl.BlockSpec((2, width), lambda b, j: (0, j))
    return pl.pallas_call(
        kern,
        grid=(nseq, ncol),
        in_specs=[
            pl.BlockSpec((seq, width), lambda b, j: (rb0 + b, j)),
            pl.BlockSpec((seq, width), lambda b, j: (rb0 + b, ncol + j)),
            pl.BlockSpec((None, 2, width), lambda b, j: (b, 0, j)),
            pl.BlockSpec((4, width), lambda b, j: (0, j)),
            pl.BlockSpec((1, width), lambda b, j: (0, j)),
            pl.BlockSpec((2, heads, HEAD_DIM, HEAD_DIM), lambda b, j: (0, j, 0, 0)),
            vec2,
            pl.BlockSpec((2, heads, HEAD_DIM, HEAD_DIM), lambda b, j: (0, j, 0, 0)),
            vec2,
            vec2,
        ],
        out_specs=[pl.BlockSpec((seq, width), lambda b, j: (b, j)),
                   pl.BlockSpec((None, 2, width), lambda b, j: (b, 0, j))],
        out_shape=[jax.ShapeDtypeStruct((nseq * seq, d_a), BF16),
                   jax.ShapeDtypeStruct((nseq, 2, d_a), F32)],
        scratch_shapes=[pltpu.VMEM((seq + 2 * RG_PAD, width), F32), pltpu.VMEM((seq, width), F32),
                        pltpu.VMEM((seq, width), F32)],
        compiler_params=_params("parallel", "parallel"),
        name="rglru",
    )(proj, proj, h0, conv_w, conv_b.reshape(1, d_a), w_a, b_a, w_i, b_i, lam)


def _sink_column(sink_ref, first_head, grp, rows_per_head):
    rows = lax.broadcasted_iota(jnp.int32, (grp * rows_per_head, 1), 0)
    col = jnp.zeros((grp * rows_per_head, 1), F32)
    for g in range(grp):
        in_g = (rows >= g * rows_per_head) & (rows < (g + 1) * rows_per_head)
        col = jnp.where(in_g, sink_ref[first_head + g], col)
    return col


def _ctx_attn_kernel(sink_ref, q_ref, k_ref, v_ref, *rest, nkv, grp, use_sink):
    o_ref, ko_ref, vo_ref = rest[-3:]
    seq = q_ref.shape[0]
    scale = HEAD_DIM ** -0.5
    for kk in range(nkv):
        k = k_ref[:, kk * HEAD_DIM:(kk + 1) * HEAD_DIM]
        v = v_ref[:, kk * HEAD_DIM:(kk + 1) * HEAD_DIM]
        ko_ref[kk] = k
        vo_ref[kk] = v
        qs = jnp.concatenate(
            [q_ref[:, (kk * grp + g) * HEAD_DIM:(kk * grp + g + 1) * HEAD_DIM] for g in range(grp)], axis=0)
        s = _dot_nt((qs * scale).astype(BF16), k.astype(BF16))
        m = jnp.max(s, axis=-1, keepdims=True)
        if use_sink:
            first = (pl.program_id(1) * nkv + kk) * grp
            sk = _sink_column(sink_ref, first, grp, seq)
            m = jnp.maximum(m, sk)
        p = jnp.exp(s - m)
        den = jnp.sum(p, axis=-1, keepdims=True)
        if use_sink:
            den = den + jnp.exp(sk - m)
        o = _dot(p.astype(BF16), v.astype(BF16)) / den
        for g in range(grp):
            hq = kk * grp + g
            o_ref[:, hq * HEAD_DIM:(hq + 1) * HEAD_DIM] = o[g * seq:(g + 1) * seq].astype(o_ref.dtype)


def ctx_attention(proj, sink, caches, slot, n_slots, *, nseq, seq, q_col, k_col, v_col, n_kv, grp, use_sink):
    nkv = max(1, 4 // grp)
    qw = nkv * grp * HEAD_DIM
    kw = nkv * HEAD_DIM
    kern = functools.partial(_ctx_attn_kernel, nkv=nkv, grp=grp, use_sink=use_sink)
    cache_spec = pl.BlockSpec((None, None, nkv, seq, HEAD_DIM), lambda b, j: (b, slot, j, 0, 0))
    cache_shape = jax.ShapeDtypeStruct((nseq, n_slots, n_kv, seq, HEAD_DIM), F32)
    prev = [] if caches is None else list(caches)
    return pl.pallas_call(
        kern,
        grid=(nseq, n_kv // nkv),
        in_specs=[
            pl.BlockSpec(memory_space=pltpu.SMEM),
            pl.BlockSpec((seq, qw), lambda b, j: (b, q_col // qw + j)),
            pl.BlockSpec((seq, kw), lambda b, j: (b, k_col // kw + j)),
            pl.BlockSpec((seq, kw), lambda b, j: (b, v_col // kw + j)),
        ] + [pl.BlockSpec(memory_space=pl.ANY)] * len(prev),
        out_specs=[pl.BlockSpec((seq, qw), lambda b, j: (b, j)), cache_spec, cache_spec],
        out_shape=[jax.ShapeDtypeStruct((nseq * seq, n_kv * grp * HEAD_DIM), BF16), cache_shape, cache_shape],
        input_output_aliases={4: 1, 5: 2} if prev else {},
        compiler_params=_params("parallel", "parallel"),
        name="ctx_attention",
    )(sink, proj, proj, proj, *prev)


def na_bias_table(rel_bias):
    nh = rel_bias.shape[0]
    nk = 3 * NA_QROWS
    padded = jnp.pad(rel_bias.astype(F32), ((0, 0), (0, 0), (GRID_W - NA_COLS, GRID_W - NA_COLS)))
    toep = jnp.stack([padded[:, :, GRID_W - 1 - c:2 * GRID_W - 1 - c] for c in range(GRID_W)], axis=2)
    c = jnp.arange(GRID_W)[:, None]
    j = jnp.arange(GRID_W)[None, :]
    cs = jnp.clip(c - NA_COLS // 2, 0, GRID_W - NA_COLS)
    toep = jnp.where((j >= cs) & (j < cs + NA_COLS), toep, MASKED)
    d0 = NA_ROWS - 1 - NA_QROWS
    per_q = jnp.stack([toep[:, d0 - qi:d0 - qi + nk].transpose(0, 2, 1, 3) for qi in range(NA_QROWS)],
                      axis=1)
    qi = jnp.arange(NA_QROWS)[:, None]
    ki = jnp.arange(nk)[None, :]
    row_ok = jnp.stack([
        (ki >= NA_QROWS) & (ki < NA_QROWS + NA_ROWS) & (qi >= 0),
        (ki >= qi) & (ki < qi + NA_ROWS),
        (ki >= 0) & (ki < NA_ROWS) & (qi >= 0),
    ])
    tab = jnp.where(row_ok[:, None, :, None, :, None], per_q[None], MASKED)
    return tab.reshape(3, nh, NA_QROWS * GRID_W, nk * GRID_W)


NA_STEP_HEADS = 2


def _na_kernel(q_ref, kp_ref, kc_ref, kn_ref, vp_ref, vc_ref, vn_ref, kx_ref, vx_ref, bias_ref, o_ref):
    for hh in range(NA_STEP_HEADS):
        cols = slice(hh * HEAD_DIM, (hh + 1) * HEAD_DIM)
        q = (q_ref[:, cols] * HEAD_DIM ** -0.5).astype(BF16)
        kl = jnp.concatenate([kp_ref[:, cols], kc_ref[:, cols], kn_ref[:, cols]], axis=0).astype(BF16)
        vl = jnp.concatenate([vp_ref[:, cols], vc_ref[:, cols], vn_ref[:, cols]], axis=0).astype(BF16)
        s_loc = _dot_nt(q, kl) + bias_ref[hh]
        s_ctx = _dot_nt(q, kx_ref[hh].astype(BF16))
        m = jnp.maximum(jnp.max(s_loc, axis=-1, keepdims=True), jnp.max(s_ctx, axis=-1, keepdims=True))
        p_loc = jnp.exp(s_loc - m)
        p_ctx = jnp.exp(s_ctx - m)
        den = jnp.sum(p_loc, axis=-1, keepdims=True) + jnp.sum(p_ctx, axis=-1, keepdims=True)
        o = _dot(p_loc.astype(BF16), vl) + _dot(p_ctx.astype(BF16), vx_ref[hh].astype(BF16))
        o_ref[:, cols] = (o / den).astype(o_ref.dtype)


def na_attention(proj, k_ctx, v_ctx, bias_tab, *, row0, nseq, seq, q_col, k_col, v_col, n_heads):
    blk = NA_QROWS * GRID_W
    nb = seq // blk
    rb0 = row0 // blk
    past = k_ctx.shape[2]
    hs = NA_STEP_HEADS
    width = hs * HEAD_DIM
    qc, kc, vc = q_col // width, k_col // width, v_col // width

    def rows(b, i):
        return rb0 + b * nb + i

    def spec(col, delta):
        return pl.BlockSpec((blk, width),
                            lambda h, b, i: (rows(b, jnp.clip(i + delta, 0, nb - 1)), col + h))

    def block_type(i):
        return jnp.where(i == 0, 0, jnp.where(i == nb - 1, 2, 1))

    cache = pl.BlockSpec((None, hs, past, HEAD_DIM), lambda h, b, i: (b, h, 0, 0))
    return pl.pallas_call(
        _na_kernel,
        grid=(n_heads // hs, nseq, nb),
        in_specs=[spec(qc, 0), spec(kc, -1), spec(kc, 0), spec(kc, 1),
                  spec(vc, -1), spec(vc, 0), spec(vc, 1), cache, cache,
                  pl.BlockSpec((None, hs, blk, 3 * blk), lambda h, b, i: (block_type(i), h, 0, 0))],
        out_specs=pl.BlockSpec((blk, width), lambda h, b, i: (b * nb + i, h)),
        out_shape=jax.ShapeDtypeStruct((nseq * seq, n_heads * HEAD_DIM), BF16),
        compiler_params=_params("parallel", "parallel", "parallel"),
        name="na_attention",
    )(proj, proj, proj, proj, proj, proj, proj, k_ctx, v_ctx, bias_tab)


def rope_tables(seq):
    t = jnp.arange(seq)
    row = (t // GRID_W).astype(F32)
    col = (t % GRID_W).astype(F32)
    n = HEAD_DIM // 4
    inv = ROPE_BASE ** (-jnp.arange(n, dtype=F32) / n)
    ang = jnp.concatenate([row[:, None] * inv, col[:, None] * inv], axis=-1)
    cos, sin = jnp.cos(ang), jnp.sin(ang)
    return jnp.concatenate([cos, cos], axis=-1), jnp.concatenate([-sin, sin], axis=-1)


SWA_STEP_KV = 2


def _rope(x, cos, sin):
    return x * cos + pltpu.roll(x, HEAD_DIM // 2, 1) * sin


def _swa_kernel(sink_ref, q_ref, kp_ref, kc_ref, kn_ref, vp_ref, vc_ref, vn_ref, kx_ref, vx_ref,
                cos_ref, sin_ref, o_ref, *, grp, nb):
    kvh = pl.program_id(0)
    i = pl.program_id(2)
    blk = SWA_BLOCK

    def tables(ib):
        t0 = pl.multiple_of(ib * blk, blk)
        return cos_ref[pl.ds(t0, blk), :], sin_ref[pl.ds(t0, blk), :]

    cq, sq = tables(i)
    cp, sp = tables(jnp.maximum(i - 1, 0))
    cn, sn = tables(jnp.minimum(i + 1, nb - 1))
    scale = HEAD_DIM ** -0.5
    qi = lax.broadcasted_iota(jnp.int32, (blk, 3 * blk), 0)
    kj = lax.broadcasted_iota(jnp.int32, (blk, 3 * blk), 1)
    ok = (jnp.abs(kj - blk - qi) <= SWA_WINDOW)
    ok = ok & ((kj >= blk) | (i > 0)) & ((kj < 2 * blk) | (i < nb - 1))
    okg = jnp.concatenate([ok] * grp, axis=0)
    for kk in range(SWA_STEP_KV):
        kcols = slice(kk * HEAD_DIM, (kk + 1) * HEAD_DIM)
        q0 = kk * grp
        qs = jnp.concatenate(
            [_rope(q_ref[:, (q0 + g) * HEAD_DIM:(q0 + g + 1) * HEAD_DIM], cq, sq) * scale for g in range(grp)],
            axis=0).astype(BF16)
        kl = jnp.concatenate([_rope(kp_ref[:, kcols], cp, sp), _rope(kc_ref[:, kcols], cq, sq),
                              _rope(kn_ref[:, kcols], cn, sn)], axis=0).astype(BF16)
        vl = jnp.concatenate([vp_ref[:, kcols], vc_ref[:, kcols], vn_ref[:, kcols]], axis=0).astype(BF16)
        s_loc = jnp.where(okg, _dot_nt(qs, kl), MASKED)
        s_ctx = _dot_nt(qs, kx_ref[kk].astype(BF16))
        sk = _sink_column(sink_ref, (kvh * SWA_STEP_KV + kk) * grp, grp, blk)
        m = jnp.maximum(jnp.maximum(jnp.max(s_loc, axis=-1, keepdims=True),
                                    jnp.max(s_ctx, axis=-1, keepdims=True)), sk)
        p_loc = jnp.exp(s_loc - m)
        p_ctx = jnp.exp(s_ctx - m)
        den = (jnp.sum(p_loc, axis=-1, keepdims=True) + jnp.sum(p_ctx, axis=-1, keepdims=True)
               + jnp.exp(sk - m))
        o = (_dot(p_loc.astype(BF16), vl) + _dot(p_ctx.astype(BF16), vx_ref[kk].astype(BF16))) / den
        for g in range(grp):
            o_ref[:, (q0 + g) * HEAD_DIM:(q0 + g + 1) * HEAD_DIM] = o[g * blk:(g + 1) * blk].astype(o_ref.dtype)


def swa_attention(proj, k_ctx, v_ctx, sink, cos2, sin2, *, row0, nseq, seq, q_col, k_col, v_col, n_kv, grp):
    blk = SWA_BLOCK
    nb = seq // blk
    rb0 = row0 // blk
    past = k_ctx.shape[2]
    kw = SWA_STEP_KV * HEAD_DIM
    qw = grp * kw
    qc, kc, vc = q_col // qw, k_col // kw, v_col // kw

    def kvspec(col, delta):
        return pl.BlockSpec((blk, kw),
                            lambda h, b, i: (rb0 + b * nb + jnp.clip(i + delta, 0, nb - 1), col + h))

    cache = pl.BlockSpec((None, SWA_STEP_KV, past, HEAD_DIM), lambda h, b, i: (b, h, 0, 0))
    table = pl.BlockSpec((seq, HEAD_DIM), lambda h, b, i: (0, 0))
    kern = functools.partial(_swa_kernel, grp=grp, nb=nb)
    return pl.pallas_call(
        kern,
        grid=(n_kv // SWA_STEP_KV, nseq, nb),
        in_specs=[pl.BlockSpec(memory_space=pltpu.SMEM),
                  pl.BlockSpec((blk, qw), lambda h, b, i: (rb0 + b * nb + i, qc + h)),
                  kvspec(kc, -1), kvspec(kc, 0), kvspec(kc, 1),
                  kvspec(vc, -1), kvspec(vc, 0), kvspec(vc, 1),
                  cache, cache, table, table],
        out_specs=pl.BlockSpec((blk, qw), lambda h, b, i: (b * nb + i, h)),
        out_shape=jax.ShapeDtypeStruct((nseq * seq, n_kv * grp * HEAD_DIM), BF16),
        compiler_params=_params("parallel", "parallel", "parallel"),
        name="swa_attention",
    )(sink, proj, proj, proj, proj, proj, proj, proj, k_ctx, v_ctx, cos2, sin2)


CONV_CHUNK = 256
CONV_PAD = 8


def _conv_silu_kernel(x_ref, w_ref, b_ref, o_ref, xpad, *, seq):
    chunk = min(CONV_CHUNK, seq)
    width = x_ref.shape[1]
    xpad[0:CONV_PAD, :] = jnp.zeros((CONV_PAD, width), F32)
    xpad[CONV_PAD + seq:, :] = jnp.zeros((CONV_PAD, width), F32)
    xpad[CONV_PAD:CONV_PAD + seq, :] = x_ref[...]
    win_rows = chunk + 2 * CONV_PAD

    def body(c, carry):
        t0 = pl.multiple_of(c * chunk, chunk)
        win = xpad[pl.ds(t0, win_rows), :]

        def at(off):
            return pltpu.roll(win, (win_rows - off) % win_rows, 0)[CONV_PAD:CONV_PAD + chunk]

        y = (w_ref[0:1, :] * at(-2) + w_ref[1:2, :] * at(-1) + w_ref[2:3, :] * at(0)
             + w_ref[3:4, :] * at(1) + b_ref[...])
        o_ref[pl.ds(t0, chunk), :] = _silu(y)
        return carry

    lax.fori_loop(0, seq // chunk, body, 0)


def conv_silu(proj, w, b, *, row0, nseq, seq, col0):
    width = w.shape[1]
    tc = 256
    rb0 = row0 // seq
    return pl.pallas_call(
        functools.partial(_conv_silu_kernel, seq=seq),
        grid=(nseq, width // tc),
        in_specs=[pl.BlockSpec((seq, tc), lambda b, j: (rb0 + b, col0 // tc + j)),
                  pl.BlockSpec((4, tc), lambda b, j: (0, j)),
                  pl.BlockSpec((1, tc), lambda b, j: (0, j))],
        out_specs=pl.BlockSpec((seq, tc), lambda b, j: (b, j)),
        out_shape=jax.ShapeDtypeStruct((nseq * seq, width), F32),
        scratch_shapes=[pltpu.VMEM((seq + 2 * CONV_PAD, tc), F32)],
        compiler_params=_params("parallel", "parallel"),
        name="conv_silu",
    )(proj, w, b.reshape(1, width))


def _ssd_kernel(x_ref, b_ref, c_ref, dtc_ref, dtr_ref, plane_ref, psub_ref, h0_ref,
                y_ref, st_ref, ht, xw, *, rev, nchunks, n_state):
    c = pl.program_id(1)
    q = SSD_CHUNK
    gh = SSD_GROUP_HEADS
    gw = gh * SSD_HEADDIM
    n_groups = ht.shape[0]
    hi = lax.Precision.HIGHEST

    @pl.when(c == 0)
    def _():
        ht[...] = h0_ref[...]

    ii = lax.broadcasted_iota(jnp.int32, (q, q), 0)
    jj = lax.broadcasted_iota(jnp.int32, (q, q), 1)
    tri = (jj >= ii) if rev else (jj <= ii)
    tri_f = tri.astype(F32)
    lane = lax.broadcasted_iota(jnp.int32, (q, 2 * SSD_HEADDIM), 1)
    lane1 = lane[0:1, :]
    last = 0 if rev else q - 1

    def pair(cols, p, lanes):
        rows = cols.shape[0]
        lo = jnp.broadcast_to(cols[:, 2 * p:2 * p + 1], (rows, 2 * SSD_HEADDIM))
        hi_ = jnp.broadcast_to(cols[:, 2 * p + 1:2 * p + 2], (rows, 2 * SSD_HEADDIM))
        return jnp.where(lanes < SSD_HEADDIM, lo, hi_)

    dt_c = _softplus(dtc_ref[...] + plane_ref[1:2, :])
    dt_r = _softplus(dtr_ref[...] + psub_ref[:, 1:2])
    acs_c = jnp.dot(tri_f, dt_c * -jnp.exp(plane_ref[0:1, :]), precision=hi, preferred_element_type=F32)
    acs_r = lax.dot_general(dt_r * -jnp.exp(psub_ref[:, 0:1]), tri_f, NT_DIMS, precision=hi,
                            preferred_element_type=F32)
    acs_last = acs_c[last:last + 1, :]
    to_end = jnp.exp(acs_last - acs_c) * dt_c
    eac = jnp.exp(acs_c)
    chunk_decay = jnp.exp(acs_last)
    for gg in range(n_groups):
        bm = b_ref[:, gg * n_state:(gg + 1) * n_state].astype(BF16)
        cm = c_ref[:, gg * n_state:(gg + 1) * n_state].astype(BF16)
        cb = _dot_nt(cm, bm)
        y_off = _dot(cm, ht[gg].astype(BF16))
        decay_parts = []
        for p in range(gh // 2):
            hp = gg * (gh // 2) + p
            cols = slice(hp * 2 * SSD_HEADDIM, (hp + 1) * 2 * SSD_HEADDIM)
            pcols = slice(p * 2 * SSD_HEADDIM, (p + 1) * 2 * SSD_HEADDIM)
            xp = x_ref[:, cols]
            yp = y_off[:, pcols] * pair(eac, hp, lane)
            for half in range(2):
                h = 2 * hp + half
                seg = jnp.where(tri, acs_c[:, h:h + 1] - acs_r[h:h + 1, :], MASKED)
                w = cb * jnp.exp(seg) * dt_r[h:h + 1, :]
                in_half = (lane >= SSD_HEADDIM) if half else (lane < SSD_HEADDIM)
                yp = yp + _dot(w.astype(BF16), jnp.where(in_half, xp, 0.0).astype(BF16))
            y_ref[:, cols] = yp
            xw[:, cols] = (xp * pair(to_end, hp, lane)).astype(BF16)
            decay_parts.append(pair(chunk_decay, hp, lane1))
        st = lax.dot_general(bm, xw[:, gg * gw:(gg + 1) * gw], TN_DIMS, preferred_element_type=F32)
        ht[gg] = ht[gg] * jnp.concatenate(decay_parts, axis=1) + st

    @pl.when(c == nchunks - 1)
    def _():
        for gg in range(n_groups):
            st_ref[gg] = ht[gg].T


def ssd_scan(xbc, dt_col, dt_row, plane, psub, h0t, *, row0, nseq, seq, rev, d_ssd, n_groups, d_state):
    q = SSD_CHUNK
    nchunks = seq // q
    width = SSD_GROUP_HEADS * SSD_HEADDIM
    n_heads = n_groups * SSD_GROUP_HEADS
    rb0 = row0 // q
    gs = n_groups * d_state
    assert d_ssd % gs == 0 and n_groups * width == d_ssd and n_heads <= 128

    def chunk(c):
        return (nchunks - 1 - c) if rev else c

    kern = functools.partial(_ssd_kernel, rev=rev, nchunks=nchunks, n_state=d_state)
    return pl.pallas_call(
        kern,
        grid=(nseq, nchunks),
        in_specs=[
            pl.BlockSpec((q, d_ssd), lambda b, c: (b * nchunks + chunk(c), 0)),
            pl.BlockSpec((q, gs), lambda b, c: (b * nchunks + chunk(c), d_ssd // gs)),
            pl.BlockSpec((q, gs), lambda b, c: (b * nchunks + chunk(c), d_ssd // gs + 1)),
            pl.BlockSpec((q, 128), lambda b, c: (rb0 + b * nchunks + chunk(c), 0)),
            pl.BlockSpec((n_heads, q), lambda b, c: (0, rb0 + b * nchunks + chunk(c))),
            pl.BlockSpec((8, 128), lambda b, c: (0, 0)),
            pl.BlockSpec((n_heads, 128), lambda b, c: (0, 0)),
            pl.BlockSpec((None, n_groups, d_state, width), lambda b, c: (b, 0, 0, 0)),
        ],
        out_specs=[
            pl.BlockSpec((q, d_ssd), lambda b, c: (b * nchunks + chunk(c), 0)),
            pl.BlockSpec((None, n_groups, width, d_state), lambda b, c: (b, 0, 0, 0)),
        ],
        out_shape=[jax.ShapeDtypeStruct((nseq * seq, d_ssd), F32),
                   jax.ShapeDtypeStruct((nseq, n_groups, width, d_state), F32)],
        scratch_shapes=[pltpu.VMEM((n_groups, d_state, width), F32), pltpu.VMEM((q, d_ssd), BF16)],
        compiler_params=_params("parallel", "arbitrary"),
        name="ssd_scan",
    )(xbc, xbc, xbc, dt_col, dt_row, plane, psub, h0t)


def _ssd_out_kernel(yf_ref, yb_ref, x_ref, d_ref, g_ref, *rest):
    z_refs, o_ref = rest[:-1], rest[-1]
    z = jnp.concatenate([r[...] for r in z_refs], axis=1)
    y = yf_ref[...] + yb_ref[...] + d_ref[...] * x_ref[...]
    o_ref[...] = _rms(y * _silu(z), g_ref[...]).astype(o_ref.dtype)


def ssd_out(yf, yb, xbc, proj, d_vec, norm_g, *, row0, z_col):
    m, d_ssd = yf.shape
    tm = EW_ROWS
    rb0 = row0 // tm
    zw = math.gcd(z_col, d_ssd)
    row = pl.BlockSpec((tm, d_ssd), lambda i: (i, 0))
    vec = pl.BlockSpec((1, d_ssd), lambda i: (0, 0))
    z_specs = [pl.BlockSpec((tm, zw), lambda i, p=p: (rb0 + i, z_col // zw + p)) for p in range(d_ssd // zw)]
    return pl.pallas_call(
        _ssd_out_kernel,
        grid=(m // tm,),
        in_specs=[row, row, row, vec, vec] + z_specs,
        out_specs=row,
        out_shape=jax.ShapeDtypeStruct((m, d_ssd), BF16),
        compiler_params=_params("parallel"),
        name="ssd_out",
    )(yf, yb, xbc, d_vec.reshape(1, d_ssd), norm_g.reshape(1, d_ssd), *([proj] * (d_ssd // zw)))


def ssd_params(a_log, dt_bias):
    nh = a_log.shape[1]
    plane = jnp.zeros((2, 8, 128), F32).at[:, 0, :nh].set(a_log).at[:, 1, :nh].set(dt_bias)
    psub = jnp.zeros((2, nh, 128), F32).at[:, :, 0].set(a_log).at[:, :, 1].set(dt_bias)
    return plane, psub


def ssd_mixer(proj, dt_col, dt_row, h0, conv_w, conv_b, a_log, dt_bias, d_skip, norm_g, *,
              row0, nseq, seq, z_col, xbc_col, d_ssd, n_groups, d_state):
    width = SSD_GROUP_HEADS * SSD_HEADDIM
    xbc = conv_silu(proj, conv_w, conv_b, row0=row0, nseq=nseq, seq=seq, col0=xbc_col)
    plane, psub = ssd_params(a_log, dt_bias)
    if h0 is None:
        h0t = jnp.zeros((2, nseq, n_groups, d_state, width), F32)
    else:
        h0t = h0.astype(F32).reshape(nseq, 2, n_groups, width, d_state).transpose(1, 0, 2, 4, 3)
    ys, sts = [], []
    for d in range(2):
        y, st = ssd_scan(xbc, dt_col, dt_row, plane[d], psub[d], h0t[d], row0=row0, nseq=nseq, seq=seq,
                         rev=bool(d), d_ssd=d_ssd, n_groups=n_groups, d_state=d_state)
        ys.append(y)
        sts.append(st.reshape(nseq, n_groups * SSD_GROUP_HEADS, SSD_HEADDIM, d_state))
    d_vec = jnp.repeat(d_skip.astype(F32), SSD_HEADDIM)
    y = ssd_out(ys[0], ys[1], xbc, proj, d_vec, norm_g, row0=row0, z_col=z_col)
    return y, jnp.stack(sts, axis=1)


def kernel(x_prompt, x_sample, cache_na_k, cache_na_v, state_rglru, cache_swa_k, cache_swa_v, state_ssd, c, c_ctx, w_mod, b_mod, norm_mix_pre, norm_mix_post, norm_ffn_pre, norm_ffn_post, w_in_even, rg_conv_w, rg_conv_b, rg_w_a, rg_b_a, rg_w_i, rg_b_i, rg_lambda, na_rel_bias, w_in_odd, swa_sink, ssd_conv_w, ssd_conv_b, ssd_a_log, ssd_dt_bias, ssd_d, ssd_norm, w_out, ffn_w_up, ffn_conv_w, ffn_conv_b, ffn_w_down):
    nb_ctx, ctx_len, d = x_prompt.shape
    nb_lat, lat_len, _ = x_sample.shape
    depth = w_mod.shape[0]
    n_ctx = nb_ctx * ctx_len
    n_lat = nb_lat * lat_len
    d_a = rg_conv_w.shape[2]
    n_heads_b = na_rel_bias.shape[1]
    d_b = n_heads_b * HEAD_DIM
    n_heads_c = swa_sink.shape[1]
    n_kv_c = cache_swa_k.shape[2]
    grp_c = n_heads_c // n_kv_c
    d_c = n_heads_c * HEAD_DIM
    d_kv = n_kv_c * HEAD_DIM
    n_ssd_heads = ssd_d.shape[1]
    d_ssd = n_ssd_heads * SSD_HEADDIM
    d_state = state_ssd.shape[-1]
    n_groups = n_ssd_heads // SSD_GROUP_HEADS
    d_xbc = ssd_conv_w.shape[2]
    d_ff = ffn_w_down.shape[1]
    d_ff_pad = -(-d_ff // 1024) * 1024
    shapes = dict(n_ctx=n_ctx, lat_len=lat_len)

    n_even, n_odd = (depth + 1) // 2, depth // 2

    x_pair = (x_prompt.reshape(n_ctx, d), x_sample.reshape(n_lat, d))
    c8 = jnp.zeros((MOD_ROWS, d), F32).at[0].set(c_ctx).at[1:1 + nb_lat].set(c)
    mod = modulation_all(c8, w_mod, b_mod)
    cos2, sin2 = rope_tables(lat_len)
    w_even = w_in_even.astype(BF16)
    w_odd = w_in_odd.astype(BF16)
    w_o = w_out.astype(BF16)
    w_up = cast_split_pad_cols(ffn_w_up, d_ff_pad)
    w_down = cast_pad_rows(ffn_w_down, d_ff_pad)
    conv = jnp.concatenate([ffn_conv_w, ffn_conv_b[:, None]], axis=1)
    conv = conv.reshape(depth, 4, 2, d_ff).transpose(0, 2, 1, 3)
    conv = jnp.pad(conv, ((0, 0), (0, 0), (0, 4), (0, d_ff_pad - d_ff)))
    rg_wa, rg_wi = rg_w_a.astype(BF16), rg_w_i.astype(BF16)

    def pad_cols(a, n):
        return jnp.pad(a, ((0, 0), (0, n - a.shape[1])))

    na_caches = swa_caches = None
    new_rglru, new_ssd = [], []
    h = adaln(x_pair, (mod, 0), norm_mix_pre[0], shift_i=0, scale_i=1, **shapes)
    for layer in range(depth):
        if layer % 2 == 0:
            e = layer // 2
            proj = matmul(h, w_even, e)
            rg = (rg_conv_w[e], rg_conv_b[e], rg_wa[e], rg_b_a[e], rg_wi[e], rg_b_i[e], rg_lambda[e])
            ya_c, st = rglru_mixer(proj, jnp.zeros((nb_ctx, 2, d_a), F32), *rg,
                                   row0=0, nseq=nb_ctx, seq=ctx_len, d_a=d_a, heads=4)
            ya_l, _ = rglru_mixer(proj, state_rglru[:, e].astype(F32), *rg,
                                  row0=n_ctx, nseq=nb_lat, seq=lat_len, d_a=d_a, heads=1)
            cols = dict(q_col=2 * d_a, k_col=2 * d_a + d_b, v_col=2 * d_a + 2 * d_b)
            yb_c, *na_caches = ctx_attention(proj, jnp.zeros((n_heads_b,), F32), na_caches, e, n_even,
                                             nseq=nb_ctx, seq=ctx_len, n_kv=n_heads_b, grp=1, use_sink=False,
                                             **cols)
            yb_l = na_attention(proj, cache_na_k[:, e], cache_na_v[:, e], na_bias_table(na_rel_bias[e]),
                                row0=n_ctx, nseq=nb_lat, seq=lat_len, n_heads=n_heads_b, **cols)
            new_rglru.append(st)
            mix_c, mix_l = (ya_c, yb_c), (ya_l, yb_l)
        else:
            od = layer // 2
            n_main = d_c + 2 * d_kv + d_ssd + d_xbc
            proj = matmul(h, w_odd, od, n=n_main)
            w_dt = w_in_odd[od][:, n_main:].astype(BF16)
            dt_col, dt_row = dt_proj(h, pad_cols(w_dt, 128), w_dt.T)
            cols = dict(q_col=0, k_col=d_c, v_col=d_c + d_kv)
            yc_c, *swa_caches = ctx_attention(proj, swa_sink[od], swa_caches, od, n_odd,
                                              nseq=nb_ctx, seq=ctx_len, n_kv=n_kv_c, grp=grp_c, use_sink=True,
                                              **cols)
            yc_l = swa_attention(proj, cache_swa_k[:, od], cache_swa_v[:, od], swa_sink[od], cos2, sin2,
                                 row0=n_ctx, nseq=nb_lat, seq=lat_len, n_kv=n_kv_c, grp=grp_c, **cols)
            ssd = (ssd_conv_w[od], ssd_conv_b[od], ssd_a_log[od], ssd_dt_bias[od], ssd_d[od], ssd_norm[od])
            scols = dict(z_col=d_c + 2 * d_kv, xbc_col=d_c + 2 * d_kv + d_ssd, d_ssd=d_ssd,
                         n_groups=n_groups, d_state=d_state)
            yd_c, st = ssd_mixer(proj, dt_col, dt_row, None, *ssd, row0=0, nseq=nb_ctx, seq=ctx_len, **scols)
            yd_l, _ = ssd_mixer(proj, dt_col, dt_row, state_ssd[:, od], *ssd,
                                row0=n_ctx, nseq=nb_lat, seq=lat_len, **scols)
            new_ssd.append(st)
            mix_c, mix_l = (yc_c, yd_c), (yc_l, yd_l)
        y_pair = (matmul_pair(*mix_c, w_o, layer), matmul_pair(*mix_l, w_o, layer))
        *x_pair, h = resid_adaln(x_pair, y_pair, (mod, layer), norm_mix_post[layer], norm_ffn_pre[layer],
                                 (mod, layer), gate_i=2, shift_i=3, scale_i=4, **shapes)
        t = ffn_up(h, w_up, conv, layer, n_ctx=n_ctx, ctx_len=ctx_len, lat_len=lat_len)
        f = matmul_ktiled(t, w_down, layer, tk=d_ff_pad // 4)
        nxt = min(layer + 1, depth - 1)
        *x_pair, h = resid_adaln(x_pair, (f, f), (mod, layer), norm_ffn_post[layer], norm_mix_pre[nxt],
                                 (mod, nxt), gate_i=5, shift_i=0, scale_i=1, **shapes)
    return (x_pair[0].reshape(nb_ctx, ctx_len, d), x_pair[1].reshape(nb_lat, lat_len, d),
            na_caches[0], na_caches[1], jnp.stack(new_rglru, axis=1),
            swa_caches[0], swa_caches[1], jnp.stack(new_ssd, axis=1))
```

```python
import functools
import math

import jax
import jax.numpy as jnp
from jax import lax
from jax.experimental import pallas as pl
from jax.experimental.pallas import tpu as pltpu

F32 = jnp.float32
BF16 = jnp.bfloat16

RMS_EPS = 1e-6
HEAD_DIM = 128
GRID_W = 64
NA_ROWS = 8
NA_COLS = 16
NA_QROWS = 4
SWA_BLOCK = 128
SWA_WINDOW = 128
ROPE_BASE = 10000.0
RG_C = 8.0
SSD_CHUNK = 128
SSD_HEADDIM = 64
SSD_GROUP_HEADS = 8
MASKED = -1e30
MOD_ROWS = 8
V7X_VMEM_BYTES = 64 * 1024 * 1024
VMEM_LIMIT = V7X_VMEM_BYTES - 8 * 1024 * 1024
ROW_TILE = 1024
COL_TILE = 1024
EW_ROWS = 256
NT_DIMS = (((1,), (1,)), ((), ()))
TN_DIMS = (((0,), (0,)), ((), ()))


def _params(*sem):
    return pltpu.CompilerParams(dimension_semantics=sem, vmem_limit_bytes=VMEM_LIMIT)


def _sigmoid(x):
    return 0.5 * jnp.tanh(0.5 * x) + 0.5


def _silu(x):
    return x * _sigmoid(x)


def _softplus(x):
    return jnp.maximum(x, 0.0) + jnp.log(1.0 + jnp.exp(-jnp.abs(x)))


def _gelu_tanh(x):
    return 0.5 * x * (1.0 + jnp.tanh(math.sqrt(2.0 / math.pi) * (x + 0.044715 * (x * x * x))))


def _rms(x, g):
    ms = jnp.mean(x * x, axis=-1, keepdims=True)
    return x * lax.rsqrt(ms + RMS_EPS) * g


def _dot(a, b):
    return jnp.dot(a, b, preferred_element_type=F32)


def _dot_nt(a, b):
    return lax.dot_general(a, b, NT_DIMS, preferred_element_type=F32)


def _mod_kernel(c_ref, w_ref, b_ref, o_ref):
    s = _silu(c_ref[...])
    o_ref[...] = _dot(s.astype(BF16), w_ref[...].astype(BF16)) + b_ref[...]


def modulation_all(c8, w_mod, b_mod):
    depth, d, n6 = w_mod.shape
    tn = 512
    per = d // tn
    return pl.pallas_call(
        _mod_kernel,
        grid=(depth, n6 // tn),
        in_specs=[
            pl.BlockSpec((MOD_ROWS, d), lambda l, j: (0, 0)),
            pl.BlockSpec((None, d, tn), lambda l, j: (l, 0, j)),
            pl.BlockSpec((None, 1, tn), lambda l, j: (l, 0, j)),
        ],
        out_specs=pl.BlockSpec((None, None, MOD_ROWS, tn), lambda l, j: (l, j // per, 0, j % per)),
        out_shape=jax.ShapeDtypeStruct((depth, 6, MOD_ROWS, d), F32),
        compiler_params=_params("parallel", "parallel"),
        name="modulation",
    )(c8, w_mod, b_mod.reshape(depth, 1, n6))


def _mod_row(i, tm, n_ctx, lat_len):
    row0 = i * tm
    return jnp.where(row0 < n_ctx, 0, 1 + (row0 - n_ctx) // lat_len)


def _group_specs(pair, tm, d, nc):
    lat0 = nc if pair[0] is pair[1] else 0
    return [pl.BlockSpec((tm, d), lambda i: (jnp.minimum(i, nc - 1), 0)),
            pl.BlockSpec((tm, d), lambda i: (lat0 + jnp.maximum(i - nc, 0), 0))]


def _adaln_kernel(xc_ref, xl_ref, mod_ref, g_ref, h_ref, *, tm, n_ctx, lat_len, shift_i, scale_i):
    i = pl.program_id(0)
    r = _mod_row(i, tm, n_ctx, lat_len)
    x = jnp.where(i * tm < n_ctx, xc_ref[...], xl_ref[...])
    shift = mod_ref[shift_i, pl.ds(r, 1), :]
    scale = mod_ref[scale_i, pl.ds(r, 1), :]
    h_ref[...] = (_rms(x, g_ref[...]) * (1.0 + scale) + shift).astype(h_ref.dtype)


def adaln(x_pair, mod, g, *, n_ctx, lat_len, shift_i, scale_i):
    d = x_pair[0].shape[1]
    tm = EW_ROWS
    nc = n_ctx // tm
    m = n_ctx + x_pair[1].shape[0] - (n_ctx if x_pair[0] is x_pair[1] else 0)
    kern = functools.partial(_adaln_kernel, tm=tm, n_ctx=n_ctx, lat_len=lat_len,
                             shift_i=shift_i, scale_i=scale_i)
    return pl.pallas_call(
        kern,
        grid=(m // tm,),
        in_specs=_group_specs(x_pair, tm, d, nc) + [
            pl.BlockSpec((None, 6, MOD_ROWS, d), lambda i: (mod[1], 0, 0, 0)),
            pl.BlockSpec((1, d), lambda i: (0, 0)),
        ],
        out_specs=pl.BlockSpec((tm, d), lambda i: (i, 0)),
        out_shape=jax.ShapeDtypeStruct((m, d), BF16),
        compiler_params=_params("parallel"),
        name="adaln",
    )(*x_pair, mod[0], g.reshape(1, d))


def _resid_kernel(x_ref, y_ref, mod_ref, gpost_ref, gpre_ref, modn_ref, *rest,
                  tm, latent, lat_len, gate_i, shift_i, scale_i):
    xo_ref, h_ref = rest[-2:]
    r = 1 + (pl.program_id(0) * tm) // lat_len if latent else 0
    gate = mod_ref[gate_i, pl.ds(r, 1), :]
    x = x_ref[...] + gate * _rms(y_ref[...], gpost_ref[...])
    xo_ref[...] = x
    shift = modn_ref[shift_i, pl.ds(r, 1), :]
    scale = modn_ref[scale_i, pl.ds(r, 1), :]
    h_ref[...] = (_rms(x, gpre_ref[...]) * (1.0 + scale) + shift).astype(h_ref.dtype)


def _resid_group(x, y, y_row0, h_prev, h_row0, m, mod, gpost, gpre, modn, *, latent, lat_len,
                 gate_i, shift_i, scale_i):
    rows, d = x.shape
    tm = EW_ROWS
    y_off, h_off = y_row0 // tm, h_row0 // tm
    kern = functools.partial(_resid_kernel, tm=tm, latent=latent, lat_len=lat_len,
                             gate_i=gate_i, shift_i=shift_i, scale_i=scale_i)
    vec = pl.BlockSpec((1, d), lambda i: (0, 0))

    def modspec(layer):
        return pl.BlockSpec((None, 6, MOD_ROWS, d), lambda i: (layer, 0, 0, 0))

    prev = [] if h_prev is None else [h_prev]
    return pl.pallas_call(
        kern,
        grid=(rows // tm,),
        in_specs=[pl.BlockSpec((tm, d), lambda i: (i, 0)),
                  pl.BlockSpec((tm, d), lambda i: (i + y_off, 0)),
                  modspec(mod[1]), vec, vec, modspec(modn[1])] + [pl.BlockSpec(memory_space=pl.ANY)] * len(prev),
        out_specs=[pl.BlockSpec((tm, d), lambda i: (i, 0)),
                   pl.BlockSpec((tm, d), lambda i: (i + h_off, 0))],
        out_shape=[jax.ShapeDtypeStruct((rows, d), F32), jax.ShapeDtypeStruct((m, d), BF16)],
        input_output_aliases={6: 1} if prev else {},
        compiler_params=_params("parallel"),
        name="resid_adaln",
    )(x, y, mod[0], gpost.reshape(1, d), gpre.reshape(1, d), modn[0], *prev)


def resid_adaln(x_pair, y_pair, mod, gpost, gpre, modn, *, n_ctx, lat_len, gate_i, shift_i, scale_i):
    n_lat = x_pair[1].shape[0]
    m = n_ctx + n_lat
    lat_row0 = n_ctx if y_pair[0] is y_pair[1] else 0
    common = dict(lat_len=lat_len, gate_i=gate_i, shift_i=shift_i, scale_i=scale_i)
    xc, h = _resid_group(x_pair[0], y_pair[0], 0, None, 0, m, mod, gpost, gpre, modn, latent=False, **common)
    xl, h = _resid_group(x_pair[1], y_pair[1], lat_row0, h, n_ctx, m, mod, gpost, gpre, modn, latent=True,
                         **common)
    return xc, xl, h


def _mm_kernel(a_ref, w_ref, o_ref):
    o_ref[...] = _dot(a_ref[...], w_ref[...]).astype(o_ref.dtype)


def matmul(a, w, layer, n=None, out_dtype=F32):
    m, k = a.shape
    n = w.shape[2] if n is None else n
    tm, tn = min(ROW_TILE, m), min(COL_TILE, n)
    return pl.pallas_call(
        _mm_kernel,
        grid=(m // tm, n // tn),
        in_specs=[pl.BlockSpec((tm, k), lambda i, j: (i, 0)),
                  pl.BlockSpec((None, k, tn), lambda i, j: (layer, 0, j))],
        out_specs=pl.BlockSpec((tm, tn), lambda i, j: (i, j)),
        out_shape=jax.ShapeDtypeStruct((m, n), out_dtype),
        compiler_params=_params("parallel", "parallel"),
        name="matmul",
    )(a, w)


def _mm_pair_kernel(a1_ref, a2_ref, w1_ref, w2_ref, o_ref):
    o_ref[...] = _dot(a1_ref[...], w1_ref[...]) + _dot(a2_ref[...], w2_ref[...])


def matmul_pair(a1, a2, w, layer):
    m, k = a1.shape
    n = w.shape[2]
    assert a2.shape == a1.shape and w.shape[1] == 2 * k
    tm, tn = min(ROW_TILE, m), min(COL_TILE, n)
    return pl.pallas_call(
        _mm_pair_kernel,
        grid=(m // tm, n // tn),
        in_specs=[pl.BlockSpec((tm, k), lambda i, j: (i, 0)),
                  pl.BlockSpec((tm, k), lambda i, j: (i, 0)),
                  pl.BlockSpec((None, k, tn), lambda i, j: (layer, 0, j)),
                  pl.BlockSpec((None, k, tn), lambda i, j: (layer, 1, j))],
        out_specs=pl.BlockSpec((tm, tn), lambda i, j: (i, j)),
        out_shape=jax.ShapeDtypeStruct((m, n), F32),
        compiler_params=_params("parallel", "parallel"),
        name="matmul_pair",
    )(a1, a2, w, w)


def _mm_acc_kernel(a_ref, w_ref, o_ref):
    @pl.when(pl.program_id(2) == 0)
    def _():
        o_ref[...] = jnp.zeros_like(o_ref)

    o_ref[...] += _dot(a_ref[...], w_ref[...])


def matmul_ktiled(a, w, layer, tk):
    m, k = a.shape
    n = w.shape[2]
    tm, tn = min(ROW_TILE, m), min(COL_TILE, n)
    return pl.pallas_call(
        _mm_acc_kernel,
        grid=(m // tm, n // tn, k // tk),
        in_specs=[pl.BlockSpec((tm, tk), lambda i, j, kk: (i, kk)),
                  pl.BlockSpec((None, tk, tn), lambda i, j, kk: (layer, kk, j))],
        out_specs=pl.BlockSpec((tm, tn), lambda i, j, kk: (i, j)),
        out_shape=jax.ShapeDtypeStruct((m, n), F32),
        compiler_params=_params("parallel", "parallel", "arbitrary"),
        name="matmul_ktiled",
    )(a, w)


CAST_TILE = 256


def _cast_pad_kernel(w_ref, o_ref, *, n_real, axis):
    @pl.when(pl.program_id(axis) < n_real)
    def _():
        o_ref[...] = w_ref[...].astype(o_ref.dtype)

    @pl.when(pl.program_id(axis) >= n_real)
    def _():
        o_ref[...] = jnp.zeros_like(o_ref)


def cast_split_pad_cols(w, n_pad):
    depth, k, n2 = w.shape
    n = n2 // 2
    t = CAST_TILE
    assert n % t == 0 and n_pad % t == 0
    n_real = n // t
    kern = functools.partial(_cast_pad_kernel, n_real=n_real, axis=2)
    return pl.pallas_call(
        kern,
        grid=(depth, 2, n_pad // t),
        in_specs=[pl.BlockSpec((None, k, t), lambda l, s, j: (l, 0, s * n_real + jnp.minimum(j, n_real - 1)))],
        out_specs=pl.BlockSpec((None, None, k, t), lambda l, s, j: (l, s, 0, j)),
        out_shape=jax.ShapeDtypeStruct((depth, 2, k, n_pad), BF16),
        compiler_params=_params("parallel", "parallel", "parallel"),
        name="cast_split_pad_cols",
    )(w)


def cast_pad_rows(w, k_pad):
    depth, k, n = w.shape
    t = CAST_TILE
    assert k % t == 0 and k_pad % t == 0
    n_real = k // t
    kern = functools.partial(_cast_pad_kernel, n_real=n_real, axis=1)
    return pl.pallas_call(
        kern,
        grid=(depth, k_pad // t),
        in_specs=[pl.BlockSpec((None, t, n), lambda l, j: (l, jnp.minimum(j, n_real - 1), 0))],
        out_specs=pl.BlockSpec((None, t, n), lambda l, j: (l, j, 0)),
        out_shape=jax.ShapeDtypeStruct((depth, k_pad, n), BF16),
        compiler_params=_params("parallel", "parallel"),
        name="cast_pad_rows",
    )(w)


def _dt_kernel(h_ref, w_ref, wt_ref, oc_ref, or_ref):
    h = h_ref[...]
    oc_ref[...] = _dot(h, w_ref[...])
    or_ref[...] = _dot_nt(wt_ref[...], h)


def dt_proj(h, w_pad, w_t):
    m, k = h.shape
    nh = w_t.shape[0]
    tm = min(ROW_TILE, m)
    return pl.pallas_call(
        _dt_kernel,
        grid=(m // tm,),
        in_specs=[pl.BlockSpec((tm, k), lambda i: (i, 0)),
                  pl.BlockSpec((k, 128), lambda i: (0, 0)),
                  pl.BlockSpec((nh, k), lambda i: (0, 0))],
        out_specs=[pl.BlockSpec((tm, 128), lambda i: (i, 0)),
                   pl.BlockSpec((nh, tm), lambda i: (0, i))],
        out_shape=[jax.ShapeDtypeStruct((m, 128), F32), jax.ShapeDtypeStruct((nh, m), F32)],
        compiler_params=_params("parallel"),
        name="dt_proj",
    )(h, w_pad, w_t)


FFN_HALO = 16
FFN_PARTS = 1


def _ffn_up_kernel(a_ref, ap_ref, an_ref, wg_ref, wv_ref, cg_ref, cv_ref, o_ref, aext, *,
                   tm, n_ctx, ctx_len, lat_len):
    i = pl.program_id(0)

    @pl.when(pl.program_id(1) == 0)
    def _():
        aext[0:FFN_HALO, :] = ap_ref[...]
        aext[FFN_HALO:FFN_HALO + tm, :] = a_ref[...]
        aext[FFN_HALO + tm:, :] = an_ref[...]

    a = aext[...]
    local = lax.broadcasted_iota(jnp.int32, (tm, 1), 0)
    is_ctx = i * tm < n_ctx
    pos = jnp.where(is_ctx, lax.rem(local, ctx_len), lax.rem(i * tm - n_ctx, lat_len) + local)
    has_prev = pos != 0
    has_next = pos != jnp.where(is_ctx, ctx_len - 1, lat_len - 1)
    rows = tm + 2 * FFN_HALO

    def conv(u, c_ref, cols):
        up = pltpu.roll(u, 1, 0)[FFN_HALO:FFN_HALO + tm]
        un = pltpu.roll(u, rows - 1, 0)[FFN_HALO:FFN_HALO + tm]
        uc = u[FFN_HALO:FFN_HALO + tm]
        return (c_ref[0:1, cols] * jnp.where(has_prev, up, 0.0) + c_ref[1:2, cols] * uc
                + c_ref[2:3, cols] * jnp.where(has_next, un, 0.0) + c_ref[3:4, cols])

    part = o_ref.shape[1] // FFN_PARTS
    for p in range(FFN_PARTS):
        cols = slice(p * part, (p + 1) * part)
        g = conv(_dot(a, wg_ref[:, cols]), cg_ref, cols)
        v = conv(_dot(a, wv_ref[:, cols]), cv_ref, cols)
        o_ref[:, cols] = (_silu(g) * v).astype(o_ref.dtype)


def ffn_up(h, w, conv, layer, *, n_ctx, ctx_len, lat_len):
    m, k = h.shape
    n = w.shape[3]
    tm = min(ROW_TILE, m)
    tn = 512
    assert tm % ctx_len == 0 and lat_len % tm == 0 and n_ctx % tm == 0
    nhalo = m // FFN_HALO
    per = tm // FFN_HALO
    kern = functools.partial(_ffn_up_kernel, tm=tm, n_ctx=n_ctx, ctx_len=ctx_len, lat_len=lat_len)
    return pl.pallas_call(
        kern,
        grid=(m // tm, n // tn),
        in_specs=[
            pl.BlockSpec((tm, k), lambda i, j: (i, 0)),
            pl.BlockSpec((FFN_HALO, k), lambda i, j: (jnp.maximum(i * per - 1, 0), 0)),
            pl.BlockSpec((FFN_HALO, k), lambda i, j: (jnp.minimum((i + 1) * per, nhalo - 1), 0)),
            pl.BlockSpec((None, None, k, tn), lambda i, j: (layer, 0, 0, j)),
            pl.BlockSpec((None, None, k, tn), lambda i, j: (layer, 1, 0, j)),
            pl.BlockSpec((None, None, 8, tn), lambda i, j: (layer, 0, 0, j)),
            pl.BlockSpec((None, None, 8, tn), lambda i, j: (layer, 1, 0, j)),
        ],
        out_specs=pl.BlockSpec((tm, tn), lambda i, j: (i, j)),
        out_shape=jax.ShapeDtypeStruct((m, n), BF16),
        scratch_shapes=[pltpu.VMEM((tm + 2 * FFN_HALO, k), BF16)],
        compiler_params=_params("parallel", "arbitrary"),
        name="ffn_up",
    )(h, h, h, w, w, conv, conv)


RG_CHUNK = 256
RG_PAD = 8


def _scan_chunk(a, u, h, reverse):
    nblk = a.shape[0] // 8
    row8 = lax.broadcasted_iota(jnp.int32, (8, a.shape[1]), 0)
    outs = [None] * nblk
    for b in (range(nblk - 1, -1, -1) if reverse else range(nblk)):
        aa = a[8 * b:8 * b + 8]
        uu = u[8 * b:8 * b + 8]
        for s in (1, 2, 4):
            keep = (row8 < 8 - s) if reverse else (row8 >= s)
            shift = 8 - s if reverse else s
            a_sh = jnp.where(keep, pltpu.roll(aa, shift, 0), 1.0)
            u_sh = jnp.where(keep, pltpu.roll(uu, shift, 0), 0.0)
            uu = aa * u_sh + uu
            aa = aa * a_sh
        hb = aa * h + uu
        outs[b] = hb
        h = hb[0:1] if reverse else hb[7:8]
    return jnp.concatenate(outs, axis=0), h


def _rglru_kernel(ga_ref, xa_ref, h0_ref, cw_ref, cb_ref, wa_ref, ba_ref, wi_ref, bi_ref, lam_ref,
                  y_ref, st_ref, xpad, hf, xcs, *, seq, heads):
    chunk = min(RG_CHUNK, seq)
    nchunks = seq // chunk
    width = heads * HEAD_DIM
    xpad[0:RG_PAD, :] = jnp.zeros((RG_PAD, width), F32)
    xpad[RG_PAD + seq:, :] = jnp.zeros((RG_PAD, width), F32)
    xpad[RG_PAD:RG_PAD + seq, :] = xa_ref[...]
    win_rows = chunk + 2 * RG_PAD

    for hh in range(heads):
        lanes = slice(hh * HEAD_DIM, (hh + 1) * HEAD_DIM)
        rate = [(-RG_C) * _softplus(-lam_ref[d:d + 1, lanes]) for d in range(2)]

        def conv(t0, lanes=lanes):
            win = xpad[pl.ds(t0, win_rows), lanes]

            def at(off):
                return pltpu.roll(win, (win_rows - off) % win_rows, 0)[RG_PAD:RG_PAD + chunk]

            return (cw_ref[0:1, lanes] * at(-2) + cw_ref[1:2, lanes] * at(-1)
                    + cw_ref[2:3, lanes] * at(0) + cw_ref[3:4, lanes] * at(1) + cb_ref[0:1, lanes])

        def coeffs(xc, d, lanes=lanes, hh=hh, rate=rate):
            xb = xc.astype(BF16)
            r = _sigmoid(_dot(xb, wa_ref[d, hh]) + ba_ref[d:d + 1, lanes])
            gate_i = _sigmoid(_dot(xb, wi_ref[d, hh]) + bi_ref[d:d + 1, lanes])
            a = jnp.exp(r * rate[d])
            u = jnp.sqrt(1.0 - a * a) * (gate_i * xc)
            return a, u

        def fwd(c, h, lanes=lanes, conv=conv, coeffs=coeffs):
            t0 = pl.multiple_of(c * chunk, chunk)
            xc = conv(t0)
            xcs[pl.ds(t0, chunk), lanes] = xc
            a, u = coeffs(xc, 0)
            hs, h = _scan_chunk(a, u, h, reverse=False)
            hf[pl.ds(t0, chunk), lanes] = hs
            return h

        h_f = lax.fori_loop(0, nchunks, fwd, h0_ref[0:1, lanes])

        def bwd(cc, h, lanes=lanes, coeffs=coeffs):
            t0 = pl.multiple_of((nchunks - 1 - cc) * chunk, chunk)
            a, u = coeffs(xcs[pl.ds(t0, chunk), lanes], 1)
            hs, h = _scan_chunk(a, u, h, reverse=True)
            y = (hf[pl.ds(t0, chunk), lanes] + hs) * _gelu_tanh(ga_ref[pl.ds(t0, chunk), lanes])
            y_ref[pl.ds(t0, chunk), lanes] = y.astype(y_ref.dtype)
            return h

        h_b = lax.fori_loop(0, nchunks, bwd, h0_ref[1:2, lanes])
        st_ref[0:1, lanes] = h_f
        st_ref[1:2, lanes] = h_b


def rglru_mixer(proj, h0, conv_w, conv_b, w_a, b_a, w_i, b_i, lam, *, row0, nseq, seq, d_a, heads):
    width = heads * HEAD_DIM
    ncol = d_a // width
    rb0 = row0 // seq
    kern = functools.partial(_rglru_kernel, seq=seq, heads=heads)
    vec2 = pl.BlockSpec((2, width), lambda b, j: (0, j))
    return pl.pallas_call(
        kern,
        grid=(nseq, ncol),
        in_specs=[
            pl.BlockSpec((seq, width), lambda b, j: (rb0 + b, j)),
            pl.BlockSpec((seq, width), lambda b, j: (rb0 + b, ncol + j)),
            pl.BlockSpec((None, 2, width), lambda b, j: (b, 0, j)),
            pl.BlockSpec((4, width), lambda b, j: (0, j)),
            pl.BlockSpec((1, width), lambda b, j: (0, j)),
            pl.BlockSpec((2, heads, HEAD_DIM, HEAD_DIM), lambda b, j: (0, j, 0, 0)),
            vec2,
            pl.BlockSpec((2, heads, HEAD_DIM, HEAD_DIM), lambda b, j: (0, j, 0, 0)),
            vec2,
            vec2,
        ],
        out_specs=[pl.BlockSpec((seq, width), lambda b, j: (b, j)),
                   pl.BlockSpec((None, 2, width), lambda b, j: (b, 0, j))],
        out_shape=[jax.ShapeDtypeStruct((nseq * seq, d_a), BF16),
                   jax.ShapeDtypeStruct((nseq, 2, d_a), F32)],
        scratch_shapes=[pltpu.VMEM((seq + 2 * RG_PAD, width), F32), pltpu.VMEM((seq, width), F32),
                        pltpu.VMEM((seq, width), F32)],
        compiler_params=_params("parallel", "parallel"),
        name="rglru",
    )(proj, proj, h0, conv_w, conv_b.reshape(1, d_a), w_a, b_a, w_i, b_i, lam)


def _sink_column(sink_ref, first_head, grp, rows_per_head):
    rows = lax.broadcasted_iota(jnp.int32, (grp * rows_per_head, 1), 0)
    col = jnp.zeros((grp * rows_per_head, 1), F32)
    for g in range(grp):
        in_g = (rows >= g * rows_per_head) & (rows < (g + 1) * rows_per_head)
        col = jnp.where(in_g, sink_ref[first_head + g], col)
    return col


def _ctx_attn_kernel(sink_ref, q_ref, k_ref, v_ref, *rest, nkv, grp, use_sink):
    o_ref, ko_ref, vo_ref = rest[-3:]
    seq = q_ref.shape[0]
    scale = HEAD_DIM ** -0.5
    for kk in range(nkv):
        k = k_ref[:, kk * HEAD_DIM:(kk + 1) * HEAD_DIM]
        v = v_ref[:, kk * HEAD_DIM:(kk + 1) * HEAD_DIM]
        ko_ref[kk] = k
        vo_ref[kk] = v
        qs = jnp.concatenate(
            [q_ref[:, (kk * grp + g) * HEAD_DIM:(kk * grp + g + 1) * HEAD_DIM] for g in range(grp)], axis=0)
        s = _dot_nt((qs * scale).astype(BF16), k.astype(BF16))
        m = jnp.max(s, axis=-1, keepdims=True)
        if use_sink:
            first = (pl.program_id(1) * nkv + kk) * grp
            sk = _sink_column(sink_ref, first, grp, seq)
            m = jnp.maximum(m, sk)
        p = jnp.exp(s - m)
        den = jnp.sum(p, axis=-1, keepdims=True)
        if use_sink:
            den = den + jnp.exp(sk - m)
        o = _dot(p.astype(BF16), v.astype(BF16)) / den
        for g in range(grp):
            hq = kk * grp + g
            o_ref[:, hq * HEAD_DIM:(hq + 1) * HEAD_DIM] = o[g * seq:(g + 1) * seq].astype(o_ref.dtype)


def ctx_attention(proj, sink, caches, slot, n_slots, *, nseq, seq, q_col, k_col, v_col, n_kv, grp, use_sink):
    nkv = max(1, 4 // grp)
    qw = nkv * grp * HEAD_DIM
    kw = nkv * HEAD_DIM
    kern = functools.partial(_ctx_attn_kernel, nkv=nkv, grp=grp, use_sink=use_sink)
    cache_spec = pl.BlockSpec((None, None, nkv, seq, HEAD_DIM), lambda b, j: (b, slot, j, 0, 0))
    cache_shape = jax.ShapeDtypeStruct((nseq, n_slots, n_kv, seq, HEAD_DIM), F32)
    prev = [] if caches is None else list(caches)
    return pl.pallas_call(
        kern,
        grid=(nseq, n_kv // nkv),
        in_specs=[
            pl.BlockSpec(memory_space=pltpu.SMEM),
            pl.BlockSpec((seq, qw), lambda b, j: (b, q_col // qw + j)),
            pl.BlockSpec((seq, kw), lambda b, j: (b, k_col // kw + j)),
            pl.BlockSpec((seq, kw), lambda b, j: (b, v_col // kw + j)),
        ] + [pl.BlockSpec(memory_space=pl.ANY)] * len(prev),
        out_specs=[pl.BlockSpec((seq, qw), lambda b, j: (b, j)), cache_spec, cache_spec],
        out_shape=[jax.ShapeDtypeStruct((nseq * seq, n_kv * grp * HEAD_DIM), BF16), cache_shape, cache_shape],
        input_output_aliases={4: 1, 5: 2} if prev else {},
        compiler_params=_params("parallel", "parallel"),
        name="ctx_attention",
    )(sink, proj, proj, proj, *prev)


def na_bias_table(rel_bias):
    nh = rel_bias.shape[0]
    nk = 3 * NA_QROWS
    padded = jnp.pad(rel_bias.astype(F32), ((0, 0), (0, 0), (GRID_W - NA_COLS, GRID_W - NA_COLS)))
    toep = jnp.stack([padded[:, :, GRID_W - 1 - c:2 * GRID_W - 1 - c] for c in range(GRID_W)], axis=2)
    c = jnp.arange(GRID_W)[:, None]
    j = jnp.arange(GRID_W)[None, :]
    cs = jnp.clip(c - NA_COLS // 2, 0, GRID_W - NA_COLS)
    toep = jnp.where((j >= cs) & (j < cs + NA_COLS), toep, MASKED)
    d0 = NA_ROWS - 1 - NA_QROWS
    per_q = jnp.stack([toep[:, d0 - qi:d0 - qi + nk].transpose(0, 2, 1, 3) for qi in range(NA_QROWS)],
                      axis=1)
    qi = jnp.arange(NA_QROWS)[:, None]
    ki = jnp.arange(nk)[None, :]
    row_ok = jnp.stack([
        (ki >= NA_QROWS) & (ki < NA_QROWS + NA_ROWS) & (qi >= 0),
        (ki >= qi) & (ki < qi + NA_ROWS),
        (ki >= 0) & (ki < NA_ROWS) & (qi >= 0),
    ])
    tab = jnp.where(row_ok[:, None, :, None, :, None], per_q[None], MASKED)
    return tab.reshape(3, nh, NA_QROWS * GRID_W, nk * GRID_W)


NA_STEP_HEADS = 2


def _na_kernel(q_ref, kp_ref, kc_ref, kn_ref, vp_ref, vc_ref, vn_ref, kx_ref, vx_ref, bias_ref, o_ref):
    for hh in range(NA_STEP_HEADS):
        cols = slice(hh * HEAD_DIM, (hh + 1) * HEAD_DIM)
        q = (q_ref[:, cols] * HEAD_DIM ** -0.5).astype(BF16)
        kl = jnp.concatenate([kp_ref[:, cols], kc_ref[:, cols], kn_ref[:, cols]], axis=0).astype(BF16)
        vl = jnp.concatenate([vp_ref[:, cols], vc_ref[:, cols], vn_ref[:, cols]], axis=0).astype(BF16)
        s_loc = _dot_nt(q, kl) + bias_ref[hh]
        s_ctx = _dot_nt(q, kx_ref[hh].astype(BF16))
        m = jnp.maximum(jnp.max(s_loc, axis=-1, keepdims=True), jnp.max(s_ctx, axis=-1, keepdims=True))
        p_loc = jnp.exp(s_loc - m)
        p_ctx = jnp.exp(s_ctx - m)
        den = jnp.sum(p_loc, axis=-1, keepdims=True) + jnp.sum(p_ctx, axis=-1, keepdims=True)
        o = _dot(p_loc.astype(BF16), vl) + _dot(p_ctx.astype(BF16), vx_ref[hh].astype(BF16))
        o_ref[:, cols] = (o / den).astype(o_ref.dtype)


def na_attention(proj, k_ctx, v_ctx, bias_tab, *, row0, nseq, seq, q_col, k_col, v_col, n_heads):
    blk = NA_QROWS * GRID_W
    nb = seq // blk
    rb0 = row0 // blk
    past = k_ctx.shape[2]
    hs = NA_STEP_HEADS
    width = hs * HEAD_DIM
    qc, kc, vc = q_col // width, k_col // width, v_col // width

    def rows(b, i):
        return rb0 + b * nb + i

    def spec(col, delta):
        return pl.BlockSpec((blk, width),
                            lambda h, b, i: (rows(b, jnp.clip(i + delta, 0, nb - 1)), col + h))

    def block_type(i):
        return jnp.where(i == 0, 0, jnp.where(i == nb - 1, 2, 1))

    cache = pl.BlockSpec((None, hs, past, HEAD_DIM), lambda h, b, i: (b, h, 0, 0))
    return pl.pallas_call(
        _na_kernel,
        grid=(n_heads // hs, nseq, nb),
        in_specs=[spec(qc, 0), spec(kc, -1), spec(kc, 0), spec(kc, 1),
                  spec(vc, -1), spec(vc, 0), spec(vc, 1), cache, cache,
                  pl.BlockSpec((None, hs, blk, 3 * blk), lambda h, b, i: (block_type(i), h, 0, 0))],
        out_specs=pl.BlockSpec((blk, width), lambda h, b, i: (b * nb + i, h)),
        out_shape=jax.ShapeDtypeStruct((nseq * seq, n_heads * HEAD_DIM), BF16),
        compiler_params=_params("parallel", "parallel", "parallel"),
        name="na_attention",
    )(proj, proj, proj, proj, proj, proj, proj, k_ctx, v_ctx, bias_tab)


def rope_tables(seq):
    t = jnp.arange(seq)
    row = (t // GRID_W).astype(F32)
    col = (t % GRID_W).astype(F32)
    n = HEAD_DIM // 4
    inv = ROPE_BASE ** (-jnp.arange(n, dtype=F32) / n)
    ang = jnp.concatenate([row[:, None] * inv, col[:, None] * inv], axis=-1)
    cos, sin = jnp.cos(ang), jnp.sin(ang)
    return jnp.concatenate([cos, cos], axis=-1), jnp.concatenate([-sin, sin], axis=-1)


SWA_STEP_KV = 2


def _rope(x, cos, sin):
    return x * cos + pltpu.roll(x, HEAD_DIM // 2, 1) * sin


def _swa_kernel(sink_ref, q_ref, kp_ref, kc_ref, kn_ref, vp_ref, vc_ref, vn_ref, kx_ref, vx_ref,
                cos_ref, sin_ref, o_ref, *, grp, nb):
    kvh = pl.program_id(0)
    i = pl.program_id(2)
    blk = SWA_BLOCK

    def tables(ib):
        t0 = pl.multiple_of(ib * blk, blk)
        return cos_ref[pl.ds(t0, blk), :], sin_ref[pl.ds(t0, blk), :]

    cq, sq = tables(i)
    cp, sp = tables(jnp.maximum(i - 1, 0))
    cn, sn = tables(jnp.minimum(i + 1, nb - 1))
    scale = HEAD_DIM ** -0.5
    qi = lax.broadcasted_iota(jnp.int32, (blk, 3 * blk), 0)
    kj = lax.broadcasted_iota(jnp.int32, (blk, 3 * blk), 1)
    ok = (jnp.abs(kj - blk - qi) <= SWA_WINDOW)
    ok = ok & ((kj >= blk) | (i > 0)) & ((kj < 2 * blk) | (i < nb - 1))
    okg = jnp.concatenate([ok] * grp, axis=0)
    for kk in range(SWA_STEP_KV):
        kcols = slice(kk * HEAD_DIM, (kk + 1) * HEAD_DIM)
        q0 = kk * grp
        qs = jnp.concatenate(
            [_rope(q_ref[:, (q0 + g) * HEAD_DIM:(q0 + g + 1) * HEAD_DIM], cq, sq) * scale for g in range(grp)],
            axis=0).astype(BF16)
        kl = jnp.concatenate([_rope(kp_ref[:, kcols], cp, sp), _rope(kc_ref[:, kcols], cq, sq),
                              _rope(kn_ref[:, kcols], cn, sn)], axis=0).astype(BF16)
        vl = jnp.concatenate([vp_ref[:, kcols], vc_ref[:, kcols], vn_ref[:, kcols]], axis=0).astype(BF16)
        s_loc = jnp.where(okg, _dot_nt(qs, kl), MASKED)
        s_ctx = _dot_nt(qs, kx_ref[kk].astype(BF16))
        sk = _sink_column(sink_ref, (kvh * SWA_STEP_KV + kk) * grp, grp, blk)
        m = jnp.maximum(jnp.maximum(jnp.max(s_loc, axis=-1, keepdims=True),
                                    jnp.max(s_ctx, axis=-1, keepdims=True)), sk)
        p_loc = jnp.exp(s_loc - m)
        p_ctx = jnp.exp(s_ctx - m)
        den = (jnp.sum(p_loc, axis=-1, keepdims=True) + jnp.sum(p_ctx, axis=-1, keepdims=True)
               + jnp.exp(sk - m))
        o = (_dot(p_loc.astype(BF16), vl) + _dot(p_ctx.astype(BF16), vx_ref[kk].astype(BF16))) / den
        for g in range(grp):
            o_ref[:, (q0 + g) * HEAD_DIM:(q0 + g + 1) * HEAD_DIM] = o[g * blk:(g + 1) * blk].astype(o_ref.dtype)


def swa_attention(proj, k_ctx, v_ctx, sink, cos2, sin2, *, row0, nseq, seq, q_col, k_col, v_col, n_kv, grp):
    blk = SWA_BLOCK
    nb = seq // blk
    rb0 = row0 // blk
    past = k_ctx.shape[2]
    kw = SWA_STEP_KV * HEAD_DIM
    qw = grp * kw
    qc, kc, vc = q_col // qw, k_col // kw, v_col // kw

    def kvspec(col, delta):
        return pl.BlockSpec((blk, kw),
                            lambda h, b, i: (rb0 + b * nb + jnp.clip(i + delta, 0, nb - 1), col + h))

    cache = pl.BlockSpec((None, SWA_STEP_KV, past, HEAD_DIM), lambda h, b, i: (b, h, 0, 0))
    table = pl.BlockSpec((seq, HEAD_DIM), lambda h, b, i: (0, 0))
    kern = functools.partial(_swa_kernel, grp=grp, nb=nb)
    return pl.pallas_call(
        kern,
        grid=(n_kv // SWA_STEP_KV, nseq, nb),
        in_specs=[pl.BlockSpec(memory_space=pltpu.SMEM),
                  pl.BlockSpec((blk, qw), lambda h, b, i: (rb0 + b * nb + i, qc + h)),
                  kvspec(kc, -1), kvspec(kc, 0), kvspec(kc, 1),
                  kvspec(vc, -1), kvspec(vc, 0), kvspec(vc, 1),
                  cache, cache, table, table],
        out_specs=pl.BlockSpec((blk, qw), lambda h, b, i: (b * nb + i, h)),
        out_shape=jax.ShapeDtypeStruct((nseq * seq, n_kv * grp * HEAD_DIM), BF16),
        compiler_params=_params("parallel", "parallel", "parallel"),
        name="swa_attention",
    )(sink, proj, proj, proj, proj, proj, proj, proj, k_ctx, v_ctx, cos2, sin2)


CONV_CHUNK = 256
CONV_PAD = 8


def _conv_silu_kernel(x_ref, w_ref, b_ref, o_ref, xpad, *, seq):
    chunk = min(CONV_CHUNK, seq)
    width = x_ref.shape[1]
    xpad[0:CONV_PAD, :] = jnp.zeros((CONV_PAD, width), F32)
    xpad[CONV_PAD + seq:, :] = jnp.zeros((CONV_PAD, width), F32)
    xpad[CONV_PAD:CONV_PAD + seq, :] = x_ref[...]
    win_rows = chunk + 2 * CONV_PAD

    def body(c, carry):
        t0 = pl.multiple_of(c * chunk, chunk)
        win = xpad[pl.ds(t0, win_rows), :]

        def at(off):
            return pltpu.roll(win, (win_rows - off) % win_rows, 0)[CONV_PAD:CONV_PAD + chunk]

        y = (w_ref[0:1, :] * at(-2) + w_ref[1:2, :] * at(-1) + w_ref[2:3, :] * at(0)
             + w_ref[3:4, :] * at(1) + b_ref[...])
        o_ref[pl.ds(t0, chunk), :] = _silu(y)
        return carry

    lax.fori_loop(0, seq // chunk, body, 0)


def conv_silu(proj, w, b, *, row0, nseq, seq, col0):
    width = w.shape[1]
    tc = 512
    rb0 = row0 // seq
    return pl.pallas_call(
        functools.partial(_conv_silu_kernel, seq=seq),
        grid=(nseq, width // tc),
        in_specs=[pl.BlockSpec((seq, tc), lambda b, j: (rb0 + b, col0 // tc + j)),
                  pl.BlockSpec((4, tc), lambda b, j: (0, j)),
                  pl.BlockSpec((1, tc), lambda b, j: (0, j))],
        out_specs=pl.BlockSpec((seq, tc), lambda b, j: (b, j)),
        out_shape=jax.ShapeDtypeStruct((nseq * seq, width), F32),
        scratch_shapes=[pltpu.VMEM((seq + 2 * CONV_PAD, tc), F32)],
        compiler_params=_params("parallel", "parallel"),
        name="conv_silu",
    )(proj, w, b.reshape(1, width))


def _ssd_kernel(x_ref, b_ref, c_ref, dtc_ref, dtr_ref, plane_ref, psub_ref, h0_ref,
                y_ref, st_ref, ht, xw, *, rev, nchunks, n_state):
    c = pl.program_id(1)
    q = SSD_CHUNK
    gh = SSD_GROUP_HEADS
    gw = gh * SSD_HEADDIM
    n_groups = ht.shape[0]
    hi = lax.Precision.HIGHEST

    @pl.when(c == 0)
    def _():
        ht[...] = h0_ref[...]

    ii = lax.broadcasted_iota(jnp.int32, (q, q), 0)
    jj = lax.broadcasted_iota(jnp.int32, (q, q), 1)
    tri = (jj >= ii) if rev else (jj <= ii)
    tri_f = tri.astype(F32)
    lane = lax.broadcasted_iota(jnp.int32, (q, 2 * SSD_HEADDIM), 1)
    lane1 = lane[0:1, :]
    last = 0 if rev else q - 1

    def pair(cols, p, lanes):
        rows = cols.shape[0]
        lo = jnp.broadcast_to(cols[:, 2 * p:2 * p + 1], (rows, 2 * SSD_HEADDIM))
        hi_ = jnp.broadcast_to(cols[:, 2 * p + 1:2 * p + 2], (rows, 2 * SSD_HEADDIM))
        return jnp.where(lanes < SSD_HEADDIM, lo, hi_)

    dt_c = _softplus(dtc_ref[...] + plane_ref[1:2, :])
    dt_r = _softplus(dtr_ref[...] + psub_ref[:, 1:2])
    acs_c = jnp.dot(tri_f, dt_c * -jnp.exp(plane_ref[0:1, :]), precision=hi, preferred_element_type=F32)
    acs_r = lax.dot_general(dt_r * -jnp.exp(psub_ref[:, 0:1]), tri_f, NT_DIMS, precision=hi,
                            preferred_element_type=F32)
    acs_last = acs_c[last:last + 1, :]
    to_end = jnp.exp(acs_last - acs_c) * dt_c
    eac = jnp.exp(acs_c)
    chunk_decay = jnp.exp(acs_last)
    for gg in range(n_groups):
        bm = b_ref[:, gg * n_state:(gg + 1) * n_state].astype(BF16)
        cm = c_ref[:, gg * n_state:(gg + 1) * n_state].astype(BF16)
        cb = _dot_nt(cm, bm)
        y_off = _dot(cm, ht[gg].astype(BF16))
        decay_parts = []
        for p in range(gh // 2):
            hp = gg * (gh // 2) + p
            cols = slice(hp * 2 * SSD_HEADDIM, (hp + 1) * 2 * SSD_HEADDIM)
            pcols = slice(p * 2 * SSD_HEADDIM, (p + 1) * 2 * SSD_HEADDIM)
            xp = x_ref[:, cols]
            yp = y_off[:, pcols] * pair(eac, hp, lane)
            for half in range(2):
                h = 2 * hp + half
                seg = jnp.where(tri, acs_c[:, h:h + 1] - acs_r[h:h + 1, :], MASKED)
                w = cb * jnp.exp(seg) * dt_r[h:h + 1, :]
                in_half = (lane >= SSD_HEADDIM) if half else (lane < SSD_HEADDIM)
                yp = yp + _dot(w.astype(BF16), jnp.where(in_half, xp, 0.0).astype(BF16))
            y_ref[:, cols] = yp
            xw[:, cols] = (xp * pair(to_end, hp, lane)).astype(BF16)
            decay_parts.append(pair(chunk_decay, hp, lane1))
        st = lax.dot_general(bm, xw[:, gg * gw:(gg + 1) * gw], TN_DIMS, preferred_element_type=F32)
        ht[gg] = ht[gg] * jnp.concatenate(decay_parts, axis=1) + st

    @pl.when(c == nchunks - 1)
    def _():
        for gg in range(n_groups):
            st_ref[gg] = ht[gg].T


def ssd_scan(xbc, dt_col, dt_row, plane, psub, h0t, *, row0, nseq, seq, rev, d_ssd, n_groups, d_state):
    q = SSD_CHUNK
    nchunks = seq // q
    width = SSD_GROUP_HEADS * SSD_HEADDIM
    n_heads = n_groups * SSD_GROUP_HEADS
    rb0 = row0 // q
    gs = n_groups * d_state
    assert d_ssd % gs == 0 and n_groups * width == d_ssd and n_heads <= 128

    def chunk(c):
        return (nchunks - 1 - c) if rev else c

    kern = functools.partial(_ssd_kernel, rev=rev, nchunks=nchunks, n_state=d_state)
    return pl.pallas_call(
        kern,
        grid=(nseq, nchunks),
        in_specs=[
            pl.BlockSpec((q, d_ssd), lambda b, c: (b * nchunks + chunk(c), 0)),
            pl.BlockSpec((q, gs), lambda b, c: (b * nchunks + chunk(c), d_ssd // gs)),
            pl.BlockSpec((q, gs), lambda b, c: (b * nchunks + chunk(c), d_ssd // gs + 1)),
            pl.BlockSpec((q, 128), lambda b, c: (rb0 + b * nchunks + chunk(c), 0)),
            pl.BlockSpec((n_heads, q), lambda b, c: (0, rb0 + b * nchunks + chunk(c))),
            pl.BlockSpec((8, 128), lambda b, c: (0, 0)),
            pl.BlockSpec((n_heads, 128), lambda b, c: (0, 0)),
            pl.BlockSpec((None, n_groups, d_state, width), lambda b, c: (b, 0, 0, 0)),
        ],
        out_specs=[
            pl.BlockSpec((q, d_ssd), lambda b, c: (b * nchunks + chunk(c), 0)),
            pl.BlockSpec((None, n_groups, width, d_state), lambda b, c: (b, 0, 0, 0)),
        ],
        out_shape=[jax.ShapeDtypeStruct((nseq * seq, d_ssd), F32),
                   jax.ShapeDtypeStruct((nseq, n_groups, width, d_state), F32)],
        scratch_shapes=[pltpu.VMEM((n_groups, d_state, width), F32), pltpu.VMEM((q, d_ssd), BF16)],
        compiler_params=_params("parallel", "arbitrary"),
        name="ssd_scan",
    )(xbc, xbc, xbc, dt_col, dt_row, plane, psub, h0t)


def _ssd_out_kernel(yf_ref, yb_ref, x_ref, d_ref, g_ref, *rest):
    z_refs, o_ref = rest[:-1], rest[-1]
    z = jnp.concatenate([r[...] for r in z_refs], axis=1)
    y = yf_ref[...] + yb_ref[...] + d_ref[...] * x_ref[...]
    o_ref[...] = _rms(y * _silu(z), g_ref[...]).astype(o_ref.dtype)


def ssd_out(yf, yb, xbc, proj, d_vec, norm_g, *, row0, z_col):
    m, d_ssd = yf.shape
    tm = EW_ROWS
    rb0 = row0 // tm
    zw = math.gcd(z_col, d_ssd)
    row = pl.BlockSpec((tm, d_ssd), lambda i: (i, 0))
    vec = pl.BlockSpec((1, d_ssd), lambda i: (0, 0))
    z_specs = [pl.BlockSpec((tm, zw), lambda i, p=p: (rb0 + i, z_col // zw + p)) for p in range(d_ssd // zw)]
    return pl.pallas_call(
        _ssd_out_kernel,
        grid=(m // tm,),
        in_specs=[row, row, row, vec, vec] + z_specs,
        out_specs=row,
        out_shape=jax.ShapeDtypeStruct((m, d_ssd), BF16),
        compiler_params=_params("parallel"),
        name="ssd_out",
    )(yf, yb, xbc, d_vec.reshape(1, d_ssd), norm_g.reshape(1, d_ssd), *([proj] * (d_ssd // zw)))


def ssd_params(a_log, dt_bias):
    nh = a_log.shape[1]
    plane = jnp.zeros((2, 8, 128), F32).at[:, 0, :nh].set(a_log).at[:, 1, :nh].set(dt_bias)
    psub = jnp.zeros((2, nh, 128), F32).at[:, :, 0].set(a_log).at[:, :, 1].set(dt_bias)
    return plane, psub


def ssd_mixer(proj, dt_col, dt_row, h0, conv_w, conv_b, a_log, dt_bias, d_skip, norm_g, *,
              row0, nseq, seq, z_col, xbc_col, d_ssd, n_groups, d_state):
    width = SSD_GROUP_HEADS * SSD_HEADDIM
    xbc = conv_silu(proj, conv_w, conv_b, row0=row0, nseq=nseq, seq=seq, col0=xbc_col)
    plane, psub = ssd_params(a_log, dt_bias)
    if h0 is None:
        h0t = jnp.zeros((2, nseq, n_groups, d_state, width), F32)
    else:
        h0t = h0.astype(F32).reshape(nseq, 2, n_groups, width, d_state).transpose(1, 0, 2, 4, 3)
    ys, sts = [], []
    for d in range(2):
        y, st = ssd_scan(xbc, dt_col, dt_row, plane[d], psub[d], h0t[d], row0=row0, nseq=nseq, seq=seq,
                         rev=bool(d), d_ssd=d_ssd, n_groups=n_groups, d_state=d_state)
        ys.append(y)
        sts.append(st.reshape(nseq, n_groups * SSD_GROUP_HEADS, SSD_HEADDIM, d_state))
    d_vec = jnp.repeat(d_skip.astype(F32), SSD_HEADDIM)
    y = ssd_out(ys[0], ys[1], xbc, proj, d_vec, norm_g, row0=row0, z_col=z_col)
    return y, jnp.stack(sts, axis=1)


def kernel(x_prompt, x_sample, cache_na_k, cache_na_v, state_rglru, cache_swa_k, cache_swa_v, state_ssd, c, c_ctx, w_mod, b_mod, norm_mix_pre, norm_mix_post, norm_ffn_pre, norm_ffn_post, w_in_even, rg_conv_w, rg_conv_b, rg_w_a, rg_b_a, rg_w_i, rg_b_i, rg_lambda, na_rel_bias, w_in_odd, swa_sink, ssd_conv_w, ssd_conv_b, ssd_a_log, ssd_dt_bias, ssd_d, ssd_norm, w_out, ffn_w_up, ffn_conv_w, ffn_conv_b, ffn_w_down):
    nb_ctx, ctx_len, d = x_prompt.shape
    nb_lat, lat_len, _ = x_sample.shape
    depth = w_mod.shape[0]
    n_ctx = nb_ctx * ctx_len
    n_lat = nb_lat * lat_len
    d_a = rg_conv_w.shape[2]
    n_heads_b = na_rel_bias.shape[1]
    d_b = n_heads_b * HEAD_DIM
    n_heads_c = swa_sink.shape[1]
    n_kv_c = cache_swa_k.shape[2]
    grp_c = n_heads_c // n_kv_c
    d_c = n_heads_c * HEAD_DIM
    d_kv = n_kv_c * HEAD_DIM
    n_ssd_heads = ssd_d.shape[1]
    d_ssd = n_ssd_heads * SSD_HEADDIM
    d_state = state_ssd.shape[-1]
    n_groups = n_ssd_heads // SSD_GROUP_HEADS
    d_xbc = ssd_conv_w.shape[2]
    d_ff = ffn_w_down.shape[1]
    d_ff_pad = -(-d_ff // 1024) * 1024
    shapes = dict(n_ctx=n_ctx, lat_len=lat_len)

    n_even, n_odd = (depth + 1) // 2, depth // 2

    x_pair = (x_prompt.reshape(n_ctx, d), x_sample.reshape(n_lat, d))
    c8 = jnp.zeros((MOD_ROWS, d), F32).at[0].set(c_ctx).at[1:1 + nb_lat].set(c)
    mod = modulation_all(c8, w_mod, b_mod)
    cos2, sin2 = rope_tables(lat_len)
    w_even = w_in_even.astype(BF16)
    w_odd = w_in_odd.astype(BF16)
    w_o = w_out.astype(BF16)
    w_up = cast_split_pad_cols(ffn_w_up, d_ff_pad)
    w_down = cast_pad_rows(ffn_w_down, d_ff_pad)
    conv = jnp.concatenate([ffn_conv_w, ffn_conv_b[:, None]], axis=1)
    conv = conv.reshape(depth, 4, 2, d_ff).transpose(0, 2, 1, 3)
    conv = jnp.pad(conv, ((0, 0), (0, 0), (0, 4), (0, d_ff_pad - d_ff)))
    rg_wa, rg_wi = rg_w_a.astype(BF16), rg_w_i.astype(BF16)

    def pad_cols(a, n):
        return jnp.pad(a, ((0, 0), (0, n - a.shape[1])))

    na_caches = swa_caches = None
    new_rglru, new_ssd = [], []
    h = adaln(x_pair, (mod, 0), norm_mix_pre[0], shift_i=0, scale_i=1, **shapes)
    for layer in range(depth):
        if layer % 2 == 0:
            e = layer // 2
            proj = matmul(h, w_even, e)
            rg = (rg_conv_w[e], rg_conv_b[e], rg_wa[e], rg_b_a[e], rg_wi[e], rg_b_i[e], rg_lambda[e])
            ya_c, st = rglru_mixer(proj, jnp.zeros((nb_ctx, 2, d_a), F32), *rg,
                                   row0=0, nseq=nb_ctx, seq=ctx_len, d_a=d_a, heads=4)
            ya_l, _ = rglru_mixer(proj, state_rglru[:, e].astype(F32), *rg,
                                  row0=n_ctx, nseq=nb_lat, seq=lat_len, d_a=d_a, heads=1)
            cols = dict(q_col=2 * d_a, k_col=2 * d_a + d_b, v_col=2 * d_a + 2 * d_b)
            yb_c, *na_caches = ctx_attention(proj, jnp.zeros((n_heads_b,), F32), na_caches, e, n_even,
                                             nseq=nb_ctx, seq=ctx_len, n_kv=n_heads_b, grp=1, use_sink=False,
                                             **cols)
            yb_l = na_attention(proj, cache_na_k[:, e], cache_na_v[:, e], na_bias_table(na_rel_bias[e]),
                                row0=n_ctx, nseq=nb_lat, seq=lat_len, n_heads=n_heads_b, **cols)
            new_rglru.append(st)
            mix_c, mix_l = (ya_c, yb_c), (ya_l, yb_l)
        else:
            od = layer // 2
            n_main = d_c + 2 * d_kv + d_ssd + d_xbc
            proj = matmul(h, w_odd, od, n=n_main)
            w_dt = w_in_odd[od][:, n_main:].astype(BF16)
            dt_col, dt_row = dt_proj(h, pad_cols(w_dt, 128), w_dt.T)
            cols = dict(q_col=0, k_col=d_c, v_col=d_c + d_kv)
            yc_c, *swa_caches = ctx_attention(proj, swa_sink[od], swa_caches, od, n_odd,
                                              nseq=nb_ctx, seq=ctx_len, n_kv=n_kv_c, grp=grp_c, use_sink=True,
                                              **cols)
            yc_l = swa_attention(proj, cache_swa_k[:, od], cache_swa_v[:, od], swa_sink[od], cos2, sin2,
                                 row0=n_ctx, nseq=nb_lat, seq=lat_len, n_kv=n_kv_c, grp=grp_c, **cols)
            ssd = (ssd_conv_w[od], ssd_conv_b[od], ssd_a_log[od], ssd_dt_bias[od], ssd_d[od], ssd_norm[od])
            scols = dict(z_col=d_c + 2 * d_kv, xbc_col=d_c + 2 * d_kv + d_ssd, d_ssd=d_ssd,
                         n_groups=n_groups, d_state=d_state)
            yd_c, st = ssd_mixer(proj, dt_col, dt_row, None, *ssd, row0=0, nseq=nb_ctx, seq=ctx_len, **scols)
            yd_l, _ = ssd_mixer(proj, dt_col, dt_row, state_ssd[:, od], *ssd,
                                row0=n_ctx, nseq=nb_lat, seq=lat_len, **scols)
            new_ssd.append(st)
            mix_c, mix_l = (yc_c, yd_c), (yc_l, yd_l)
        y_pair = (matmul_pair(*mix_c, w_o, layer), matmul_pair(*mix_l, w_o, layer))
        *x_pair, h = resid_adaln(x_pair, y_pair, (mod, layer), norm_mix_post[layer], norm_ffn_pre[layer],
                                 (mod, layer), gate_i=2, shift_i=3, scale_i=4, **shapes)
        t = ffn_up(h, w_up, conv, layer, n_ctx=n_ctx, ctx_len=ctx_len, lat_len=lat_len)
        f = matmul_ktiled(t, w_down, layer, tk=d_ff_pad // 4)
        nxt = min(layer + 1, depth - 1)
        *x_pair, h = resid_adaln(x_pair, (f, f), (mod, layer), norm_ffn_post[layer], norm_mix_pre[nxt],
                                 (mod, nxt), gate_i=5, shift_i=0, scale_i=1, **shapes)
    return (x_pair[0].reshape(nb_ctx, ctx_len, d), x_pair[1].reshape(nb_lat, lat_len, d),
            na_caches[0], na_caches[1], jnp.stack(new_rglru, axis=1),
            swa_caches[0], swa_caches[1], jnp.stack(new_ssd, axis=1))
```
